```python
import jax, jax.numpy as jnp
from jax import lax
import numpy as np

D_MODEL = 1024
BATCH = 8
SEQ = 4096
DEPTH = 2
DEC_BATCH = 2
DEC_SEQ = 16384
PAST_LEN = 128

ROPE_THETA = 10000.0
EPS = 1e-6
NEG_INF = -1e30
MLA_HEADS = 8
MLA_NOPE = 64
MLA_ROPE = 32
MLA_V = 64
Q_LORA = 384
KV_LORA = 256
MLA_WIDTH = MLA_HEADS * MLA_V
MLA_QBLOCK = 128
DIL_GROUPS = ((128, 1), (512, 4), (2048, 16))
DIL_HEADS = 8
DIL_HEAD_DIM = 64
DIL_WIDTH = DIL_HEADS * DIL_HEAD_DIM
N_BRANCH = 2
IN_SPLITS = (Q_LORA, KV_LORA, MLA_ROPE, MLA_WIDTH) + (DIL_WIDTH,) * (3 * len(DIL_GROUPS)) + (DIL_WIDTH, N_BRANCH * D_MODEL)
IN_WIDTH = sum(IN_SPLITS)

kernel_name = "hybrid_mla_dilated_encoder"


def _split_in(u):
    offs, acc = [], 0
    for w in IN_SPLITS[:-1]:
        acc += w
        offs.append(acc)
    return jnp.split(u, offs, axis=-1)


def _rmsnorm(x, g):
    xf = x.astype(jnp.float32)
    y = xf * lax.rsqrt(jnp.mean(xf * xf, axis=-1, keepdims=True) + EPS)
    return (y * g.astype(jnp.float32)).astype(x.dtype)


def _rope(x, pos):
    d = x.shape[-1]
    inv = jnp.power(jnp.float32(ROPE_THETA), -jnp.arange(0, d, 2, dtype=jnp.float32) / d)
    ang = pos[:, None] * inv[None, :]
    cos = jnp.cos(ang)[None, :, None, :]
    sin = jnp.sin(ang)[None, :, None, :]
    xf = x.astype(jnp.float32)
    x1, x2 = xf[..., : d // 2], xf[..., d // 2:]
    return jnp.concatenate([x1 * cos - x2 * sin, x1 * sin + x2 * cos], axis=-1).astype(x.dtype)


def _mla_attention(q, k, v):
    B, S, H, dq = q.shape
    scale = dq ** -0.5
    nq = S // MLA_QBLOCK
    qb = q.reshape(B, nq, MLA_QBLOCK, H, dq).transpose(1, 0, 2, 3, 4)

    def one(qblk):
        s = jnp.einsum('bqhe,bkhe->bhqk', qblk, k, preferred_element_type=jnp.float32) * scale
        p = jax.nn.softmax(s, axis=-1)
        return jnp.einsum('bhqk,bkhe->bqhe', p, v.astype(jnp.float32)).astype(q.dtype)

    o = lax.map(one, qb)
    return o.transpose(1, 0, 2, 3, 4).reshape(B, S, H, v.shape[-1])


def _dilated_group_attention(q, k, v, window, dil):
    B, S, H, dh = q.shape
    half = window // (2 * dil)
    blk = half
    span = dil * blk
    S_pad = -(-S // span) * span
    pad = S_pad - S
    L = S_pad // dil
    nb = L // blk

    def strided(t):
        t = jnp.pad(t, ((0, 0), (0, pad), (0, 0), (0, 0)))
        t = t.reshape(B, L, dil, H, dh).transpose(0, 2, 1, 3, 4)
        return t.reshape(B, dil, nb, blk, H, dh)

    def neighbours(t):
        tp = jnp.pad(t, ((0, 0), (0, 0), (1, 1), (0, 0), (0, 0), (0, 0)))
        return jnp.concatenate([tp[:, :, :-2], tp[:, :, 1:-1], tp[:, :, 2:]], axis=3)

    qs = strided(q)
    kw = neighbours(strided(k))
    vw = neighbours(strided(v))

    valid = (jnp.arange(S_pad) < S).reshape(L, dil).T.reshape(dil, nb, blk)
    vp = jnp.pad(valid, ((0, 0), (1, 1), (0, 0)))
    kvalid = jnp.concatenate([vp[:, :-2], vp[:, 1:-1], vp[:, 2:]], axis=2)
    rel = (jnp.arange(3 * blk)[None, :] - blk) - jnp.arange(blk)[:, None]
    mask = (jnp.abs(rel) <= half)[None, None] & kvalid[:, :, None, :]

    s = jnp.einsum('bdnqhe,bdnkhe->bdnhqk', qs, kw, preferred_element_type=jnp.float32) * (dh ** -0.5)
    s = jnp.where(mask[None, :, :, None], s, NEG_INF)
    m = jnp.max(s, axis=-1, keepdims=True)
    p = jnp.exp(s - m)
    l = jnp.sum(p, axis=-1, keepdims=True)
    o = jnp.einsum('bdnhqk,bdnkhe->bdnqhe', p / l, vw.astype(jnp.float32))
    lse = (m + jnp.log(l))[..., 0]

    o = o.reshape(B, dil, L, H, dh).transpose(0, 2, 1, 3, 4).reshape(B, S_pad, H, dh)[:, :S]
    lse = lse.transpose(0, 1, 2, 4, 3).reshape(B, dil, L, H).transpose(0, 2, 1, 3).reshape(B, S_pad, H)[:, :S]
    return o, lse


def _layer(x, c, pos, w_ada, b_ada, g_norm, w_in, b_gate, g_cq, w_uq, g_ckv, w_ukv, w_pa, w_pb, w_out):
    B, S, _ = x.shape
    mod = jax.nn.silu(c) @ w_ada + b_ada
    shift, scale, gate = jnp.split(mod, 3, axis=-1)
    h = _rmsnorm(x, g_norm) * (1.0 + scale[:, None, :]) + shift[:, None, :]

    parts = _split_in(h @ w_in)
    cq, ckv, kr, z_mla = parts[0], parts[1], parts[2], parts[3]
    z_dil, merge = parts[4 + 3 * len(DIL_GROUPS)], parts[5 + 3 * len(DIL_GROUPS)]

    q = (_rmsnorm(cq, g_cq) @ w_uq).reshape(B, S, MLA_HEADS, MLA_NOPE + MLA_ROPE)
    q = jnp.concatenate([q[..., :MLA_NOPE], _rope(q[..., MLA_NOPE:], pos)], axis=-1)
    kv = (_rmsnorm(ckv, g_ckv) @ w_ukv).reshape(B, S, MLA_HEADS, MLA_NOPE + MLA_V)
    k_pe = jnp.broadcast_to(_rope(kr.reshape(B, S, 1, MLA_ROPE), pos), (B, S, MLA_HEADS, MLA_ROPE))
    k = jnp.concatenate([kv[..., :MLA_NOPE], k_pe], axis=-1)
    v = kv[..., MLA_NOPE:]
    o_mla = _mla_attention(q, k, v).reshape(B, S, MLA_WIDTH) * jax.nn.silu(z_mla)

    outs, lses = [], []
    for gi, (win, dil) in enumerate(DIL_GROUPS):
        qg, kg, vg = [t.reshape(B, S, DIL_HEADS, DIL_HEAD_DIM) for t in parts[4 + 3 * gi: 7 + 3 * gi]]
        og, lg = _dilated_group_attention(_rope(qg, pos), _rope(kg, pos), vg, win, dil)
        outs.append(og)
        lses.append(lg)
    wts = jax.nn.softmax(jnp.stack(lses, axis=0), axis=0)
    o_dil = jnp.sum(wts[..., None] * jnp.stack(outs, axis=0), axis=0).astype(x.dtype)
    o_dil = o_dil.reshape(B, S, DIL_WIDTH) * jax.nn.silu(z_dil)

    ga, gb = jnp.split(jax.nn.sigmoid(merge + b_gate), 2, axis=-1)
    y = (ga * (o_mla @ w_pa) + gb * (o_dil @ w_pb)) @ w_out
    return x + gate[:, None, :] * y


def _trunk(x, c, w_ada, b_ada, g_norm, w_in, b_gate, g_cq, w_uq, g_ckv, w_ukv, w_pa, w_pb, w_out, g_final):
    pos = jnp.arange(x.shape[1], dtype=jnp.float32)
    for l in range(DEPTH):
        x = _layer(x, c, pos, w_ada[l], b_ada[l], g_norm[l], w_in[l], b_gate[l], g_cq[l], w_uq[l],
                   g_ckv[l], w_ukv[l], w_pa[l], w_pb[l], w_out[l])
    return _rmsnorm(x, g_final)


def setup_inputs(seed: int = 0) -> dict:
    key = jax.random.key(seed)
    ks = jax.random.split(key, 20)
    f32 = jnp.float32
    nrm = lambda k, shape, s: jax.random.normal(k, shape, f32) * s
    D = D_MODEL
    return {
        "x_prompt": nrm(ks[0], (BATCH, SEQ, D), 1.0),
        "x_sample": nrm(ks[1], (DEC_BATCH, DEC_SEQ, D), 1.0),
        "c_prompt": nrm(ks[2], (BATCH, D), 1.0),
        "c_sample": nrm(ks[3], (DEC_BATCH, D), 1.0),
        "w_ada": nrm(ks[4], (DEPTH, D, 3 * D), D ** -0.5),
        "b_ada": nrm(ks[5], (DEPTH, 3 * D), 0.02),
        "g_norm": 1.0 + nrm(ks[6], (DEPTH, D), 0.02),
        "w_in": nrm(ks[7], (DEPTH, D, IN_WIDTH), D ** -0.5),
        "b_gate": nrm(ks[8], (DEPTH, N_BRANCH * D), 0.02),
        "g_cq": 1.0 + nrm(ks[9], (DEPTH, Q_LORA), 0.02),
        "w_uq": nrm(ks[10], (DEPTH, Q_LORA, MLA_HEADS * (MLA_NOPE + MLA_ROPE)), Q_LORA ** -0.5),
        "g_ckv": 1.0 + nrm(ks[11], (DEPTH, KV_LORA), 0.02),
        "w_ukv": nrm(ks[12], (DEPTH, KV_LORA, MLA_HEADS * (MLA_NOPE + MLA_V)), KV_LORA ** -0.5),
        "w_pa": nrm(ks[13], (DEPTH, MLA_WIDTH, D), MLA_WIDTH ** -0.5),
        "w_pb": nrm(ks[14], (DEPTH, DIL_WIDTH, D), DIL_WIDTH ** -0.5),
        "w_out": nrm(ks[15], (DEPTH, D, D), D ** -0.5),
        "g_final": 1.0 + nrm(ks[16], (D,), 0.02),
    }


def reference(x_prompt, x_sample, c_prompt, c_sample, w_ada, b_ada, g_norm, w_in, b_gate, g_cq, w_uq,
              g_ckv, w_ukv, w_pa, w_pb, w_out, g_final):
    y_prompt = _trunk(x_prompt, c_prompt, w_ada, b_ada, g_norm, w_in, b_gate, g_cq, w_uq, g_ckv, w_ukv,
                      w_pa, w_pb, w_out, g_final)
    y_sample = _trunk(x_sample, c_sample, w_ada, b_ada, g_norm, w_in, b_gate, g_cq, w_uq, g_ckv, w_ukv,
                      w_pa, w_pb, w_out, g_final)
    return (y_prompt, y_sample)
```

```python
import functools
import math

import numpy as np
import jax
import jax.numpy as jnp
from jax import lax
from jax.experimental import pallas as pl
from jax.experimental.pallas import tpu as pltpu

ROPE_THETA = 10000.0
EPS = 1e-6
NEG_INF = -1e30

MLA_HEADS = 8
MLA_NOPE = 64
MLA_ROPE = 32
MLA_V = 64
Q_LORA = 384
KV_LORA = 256
MLA_WIDTH = MLA_HEADS * MLA_V
DIL_GROUPS = ((128, 1), (512, 4), (2048, 16))
DIL_HEADS = 8
DIL_HEAD_DIM = 64
DIL_WIDTH = DIL_HEADS * DIL_HEAD_DIM
N_GROUPS = len(DIL_GROUPS)
BAND_HALF = 64

LANES = 128
KR_PAD = LANES
HEAD_PAD = LANES

OFF_CQ = 0
OFF_CKV = OFF_CQ + Q_LORA
OFF_KR = OFF_CKV + KV_LORA
OFF_ZM = OFF_KR + KR_PAD
OFF_DIL = OFF_ZM + MLA_WIDTH
OFF_ZD = OFF_DIL + 3 * N_GROUPS * DIL_WIDTH
OFF_MG = OFF_ZD + DIL_WIDTH
W_BIG = OFF_MG + 2 * 1024

TOKEN_TILE = 512
MLA_Q_TILE = 1024
BAND_Q_TILE = 256
BAND_SUB = 128
VMEM_LIMIT = 56 * 1024 * 1024

_f32 = jnp.float32
_bf16 = jnp.bfloat16
_NT = (((1,), (1,)), ((), ()))


def _dot(a, b):
    return jnp.dot(a, b, preferred_element_type=_f32)


def _dot_nt(a, b):
    return lax.dot_general(a, b, _NT, preferred_element_type=_f32)


def _rms(x, g):
    return x * lax.rsqrt(jnp.mean(x * x, axis=-1, keepdims=True) + EPS) * g


def _sigmoid(x):
    return 1.0 / (1.0 + jnp.exp(-x))


def _rope_pairs(x, cos, sin):
    return x * cos + pltpu.roll(x, 64, axis=1) * sin


def _ada_kernel(c_ref, w_ref, b_ref, o_ref):
    c = c_ref[...]
    a = c * _sigmoid(c)
    a_hi = a.astype(_bf16)
    a_lo = (a - a_hi.astype(_f32)).astype(_bf16)
    w = w_ref[0]
    w_hi = w.astype(_bf16)
    w_lo = (w - w_hi.astype(_f32)).astype(_bf16)
    o_ref[0] = _dot(a_hi, w_hi) + _dot(a_hi, w_lo) + _dot(a_lo, w_hi) + b_ref[0]


def _ada_call(c_all, w_ada, b_ada):
    depth, d, n = w_ada.shape
    rows = c_all.shape[0]
    nb = n // d
    return pl.pallas_call(
        _ada_kernel,
        grid=(depth, nb),
        in_specs=[
            pl.BlockSpec((rows, d), lambda l, j: (0, 0)),
            pl.BlockSpec((1, d, d), lambda l, j: (l, 0, j)),
            pl.BlockSpec((1, 1, d), lambda l, j: (l, 0, j)),
        ],
        out_specs=pl.BlockSpec((1, rows, d), lambda l, j: (l, 0, j)),
        out_shape=jax.ShapeDtypeStruct((depth, rows, n), _f32),
        compiler_params=pltpu.CompilerParams(vmem_limit_bytes=VMEM_LIMIT),
        name="adaln",
    )(c_all, w_ada, b_ada.reshape(depth, 1, n))


def _in_kernel(x_ref, sc_ref, sh_ref, gn_ref, w_ref, gcq_ref, wuq_ref, gckv_ref, wk_ref,
               e_ref, wvt_ref, bg_ref, tm_ref, td_ref,
               q_ref, k_ref, vt_ref, zm_ref, dil_ref, zd_ref, gt_ref, *, q_scale):
    x = x_ref[0]
    h = _rms(x, gn_ref[...]) * (1.0 + sc_ref[0]) + sh_ref[0]
    hb = h.astype(_bf16)

    cos_m, sin_m = tm_ref[0], tm_ref[1]
    cos_d, sin_d = td_ref[0], td_ref[1]

    cq = _dot(hb, w_ref[:, OFF_CQ:OFF_CQ + Q_LORA])
    cqn = _rms(cq, gcq_ref[...]).astype(_bf16)
    q = _dot(cqn, wuq_ref[...])
    for hd in range(MLA_HEADS):
        sl = slice(hd * HEAD_PAD, (hd + 1) * HEAD_PAD)
        q_ref[0, hd] = (_rope_pairs(q[:, sl], cos_m, sin_m) * q_scale).astype(_bf16)

    ckv = _dot(hb, w_ref[:, OFF_CKV:OFF_CKV + KV_LORA])
    ckvn = _rms(ckv, gckv_ref[...]).astype(_bf16)
    kr = _dot(hb, w_ref[:, OFF_KR:OFF_KR + KR_PAD])
    kr_hi = kr.astype(_bf16)
    kr_lo = (kr - kr_hi.astype(_f32)).astype(_bf16)
    k = _dot(ckvn, wk_ref[...]) + _dot(kr_hi, e_ref[...]) + _dot(kr_lo, e_ref[...])
    for hd in range(MLA_HEADS):
        sl = slice(hd * HEAD_PAD, (hd + 1) * HEAD_PAD)
        k_ref[0, hd] = _rope_pairs(k[:, sl], cos_m, sin_m).astype(_bf16)
    vt = _dot_nt(wvt_ref[...], ckvn)
    for hd in range(MLA_HEADS):
        vt_ref[0, hd, 0] = vt[hd * MLA_V:(hd + 1) * MLA_V, :].astype(_bf16)

    zm = _dot(hb, w_ref[:, OFF_ZM:OFF_ZM + MLA_WIDTH])
    zm_ref[0] = (zm * _sigmoid(zm)).astype(_bf16)

    for j in range(3 * N_GROUPS):
        off = OFF_DIL + j * DIL_WIDTH
        u = _dot(hb, w_ref[:, off:off + DIL_WIDTH])
        kind = j % 3
        if kind == 2:
            dil_ref[j, 0] = u.astype(_bf16)
        else:
            post = DIL_HEAD_DIM ** -0.5 if kind == 0 else 1.0
            for p in range(DIL_WIDTH // LANES):
                sl = slice(p * LANES, (p + 1) * LANES)
                dil_ref[j, 0, :, sl] = (_rope_pairs(u[:, sl], cos_d, sin_d) * post).astype(_bf16)

    zd = _dot(hb, w_ref[:, OFF_ZD:OFF_ZD + DIL_WIDTH])
    zd_ref[0] = (zd * _sigmoid(zd)).astype(_bf16)

    mg = _dot(hb, w_ref[:, OFF_MG:W_BIG]) + bg_ref[...]
    gt_ref[0] = _sigmoid(mg).astype(_bf16)


def _in_call(x, scale, shift, lw, tab_m, tab_d):
    b, s, d = x.shape
    tm = min(TOKEN_TILE, s)
    nt = s // tm
    const = lambda *shape: pl.BlockSpec(shape, lambda bi, ti: (0,) * len(shape),
                                        pipeline_mode=pl.Buffered(1))
    in_specs = [
        pl.BlockSpec((1, tm, d), lambda bi, ti: (bi, ti, 0)),
        pl.BlockSpec((1, 1, d), lambda bi, ti: (bi, 0, 0)),
        pl.BlockSpec((1, 1, d), lambda bi, ti: (bi, 0, 0)),
        const(1, d),
        const(d, W_BIG),
        const(1, Q_LORA),
        const(Q_LORA, MLA_HEADS * HEAD_PAD),
        const(1, KV_LORA),
        const(KV_LORA, MLA_HEADS * HEAD_PAD),
        const(KR_PAD, MLA_HEADS * HEAD_PAD),
        const(MLA_WIDTH, KV_LORA),
        const(1, 2 * d),
        pl.BlockSpec((2, tm, LANES), lambda bi, ti: (0, ti, 0)),
        pl.BlockSpec((2, tm, LANES), lambda bi, ti: (0, ti, 0)),
    ]
    out_shape = [
        jax.ShapeDtypeStruct((b, MLA_HEADS, s, HEAD_PAD), _bf16),
        jax.ShapeDtypeStruct((b, MLA_HEADS, s, HEAD_PAD), _bf16),
        jax.ShapeDtypeStruct((b, MLA_HEADS, nt, MLA_V, tm), _bf16),
        jax.ShapeDtypeStruct((b, s, MLA_WIDTH), _bf16),
        jax.ShapeDtypeStruct((3 * N_GROUPS, b, s, DIL_WIDTH), _bf16),
        jax.ShapeDtypeStruct((b, s, DIL_WIDTH), _bf16),
        jax.ShapeDtypeStruct((b, s, 2 * d), _bf16),
    ]
    out_specs = [
        pl.BlockSpec((1, MLA_HEADS, tm, HEAD_PAD), lambda bi, ti: (bi, 0, ti, 0)),
        pl.BlockSpec((1, MLA_HEADS, tm, HEAD_PAD), lambda bi, ti: (bi, 0, ti, 0)),
        pl.BlockSpec((1, MLA_HEADS, 1, MLA_V, tm), lambda bi, ti: (bi, 0, ti, 0, 0)),
        pl.BlockSpec((1, tm, MLA_WIDTH), lambda bi, ti: (bi, ti, 0)),
        pl.BlockSpec((3 * N_GROUPS, 1, tm, DIL_WIDTH), lambda bi, ti: (0, bi, ti, 0)),
        pl.BlockSpec((1, tm, DIL_WIDTH), lambda bi, ti: (bi, ti, 0)),
        pl.BlockSpec((1, tm, 2 * d), lambda bi, ti: (bi, ti, 0)),
    ]
    kern = functools.partial(_in_kernel, q_scale=(MLA_NOPE + MLA_ROPE) ** -0.5)
    return pl.pallas_call(
        kern,
        grid=(b, nt),
        in_specs=in_specs,
        out_specs=out_specs,
        out_shape=out_shape,
        compiler_params=pltpu.CompilerParams(vmem_limit_bytes=VMEM_LIMIT),
        name="in_proj",
    )(x, scale, shift, lw["g_norm"], lw["w_big"], lw["g_cq"], lw["w_uq"], lw["g_ckv"],
      lw["w_k"], lw["e_kr"], lw["w_vt"], lw["b_gate"], tab_m, tab_d)


def _mla_kernel(q_ref, k_ref, vt_ref, o_ref, m_ref, acc_ref, *, n_chunks, tk):
    q = q_ref[0, 0]
    tq = q.shape[0]
    m_ref[...] = jnp.full(m_ref.shape, NEG_INF, _f32)
    acc_ref[...] = jnp.zeros(acc_ref.shape, _f32)
    ones = jnp.ones((16, tk), _bf16)

    def body(c, carry):
        start = pl.multiple_of(c * tk, tk)
        kc = k_ref[0, 0, pl.ds(start, tk), :]
        st = _dot_nt(kc, q)
        m_old = m_ref[...]
        m_new = jnp.maximum(m_old, jnp.max(st, axis=0, keepdims=True))
        alpha = jnp.exp(m_old - m_new)
        p = jnp.exp(st - m_new).astype(_bf16)
        va = jnp.concatenate([vt_ref[0, 0, c], ones], axis=0)
        acc_ref[...] = alpha * acc_ref[...] + _dot(va, p)
        m_ref[...] = m_new
        return carry

    lax.fori_loop(0, n_chunks, body, 0)
    acc = acc_ref[...]
    o_ref[0, 0] = (acc[:MLA_V] / acc[MLA_V:MLA_V + 1]).astype(_bf16)


def _mla_call(q, k, vt):
    b, hh, s, _ = q.shape
    n_chunks, tk = vt.shape[2], vt.shape[4]
    tq = min(MLA_Q_TILE, s)
    kern = functools.partial(_mla_kernel, n_chunks=n_chunks, tk=tk)
    return pl.pallas_call(
        kern,
        grid=(b, hh, s // tq),
        in_specs=[
            pl.BlockSpec((1, 1, tq, HEAD_PAD), lambda bi, hi, qi: (bi, hi, qi, 0)),
            pl.BlockSpec((1, 1, s, HEAD_PAD), lambda bi, hi, qi: (bi, hi, 0, 0)),
            pl.BlockSpec((1, 1, n_chunks, MLA_V, tk), lambda bi, hi, qi: (bi, hi, 0, 0, 0)),
        ],
        out_specs=pl.BlockSpec((1, 1, MLA_V, tq), lambda bi, hi, qi: (bi, hi, 0, qi)),
        out_shape=jax.ShapeDtypeStruct((b, hh, MLA_V, s), _bf16),
        scratch_shapes=[pltpu.VMEM((1, tq), _f32), pltpu.VMEM((MLA_V + 16, tq), _f32)],
        compiler_params=pltpu.CompilerParams(vmem_limit_bytes=VMEM_LIMIT),
        name="mla_attn",
    )(q, k, vt)


def _band_kernel(q_ref, kp_ref, kc_ref, kn_ref, vp_ref, vc_ref, vn_ref, o_ref, lse_ref,
                 *, tq, seq):
    i = pl.program_id(2)
    q = q_ref[0]
    kk = jnp.concatenate([kp_ref[0], kc_ref[0], kn_ref[0]], axis=0)
    vv = jnp.concatenate([vp_ref[0], vc_ref[0], vn_ref[0]], axis=0)
    nk = BAND_SUB + 2 * BAND_HALF
    row = lax.broadcasted_iota(jnp.int32, (BAND_SUB, nk), 0)
    col = lax.broadcasted_iota(jnp.int32, (BAND_SUB, nk), 1)
    band = jnp.abs(col - BAND_HALF - row) <= BAND_HALF
    lane = lax.broadcasted_iota(jnp.int32, (1, LANES), 1)
    first_head = (lane % 64) < 32
    low_half = lane < 64
    for sb in range(tq // BAND_SUB):
        a = sb * BAND_SUB
        kidx = i * tq + (a - BAND_HALF) + col
        valid = band & (kidx >= 0) & (kidx < seq)
        for p in range(DIL_WIDTH // LANES):
            sl = slice(p * LANES, (p + 1) * LANES)
            qp = q[a:a + BAND_SUB, sl]
            kp = kk[a:a + nk, sl]
            vp = vv[a:a + nk, sl]
            res = []
            for sel in (first_head, jnp.logical_not(first_head)):
                qm = jnp.where(sel, qp, jnp.zeros_like(qp))
                sc = jnp.where(valid, _dot_nt(qm, kp), NEG_INF)
                m = jnp.max(sc, axis=-1, keepdims=True)
                e = jnp.exp(sc - m)
                l = jnp.sum(e, axis=-1, keepdims=True)
                o = _dot(e.astype(_bf16), vp) / l
                res.append((o, m + jnp.log(l)))
            o_ref[0, a:a + BAND_SUB, sl] = jnp.where(low_half, res[0][0], res[1][0]).astype(_bf16)
            lse_ref[0, a:a + BAND_SUB, sl] = jnp.where(low_half, res[0][1], res[1][1])


def _band_call(qkv, g, dil):
    _, b, s, w = qkv.shape
    seq = s // dil
    view = qkv.reshape(3 * N_GROUPS, b, seq, dil * w)
    tq = min(BAND_Q_TILE, seq)
    assert tq % BAND_SUB == 0 and seq % tq == 0
    hb = tq // BAND_HALF
    last = seq // BAND_HALF - 1

    def cur(j):
        return pl.BlockSpec((None, 1, tq, w), lambda bi, r, i: (3 * g + j, bi, i, r))

    def prev(j):
        return pl.BlockSpec((None, 1, BAND_HALF, w),
                            lambda bi, r, i: (3 * g + j, bi, jnp.maximum(i * hb - 1, 0), r))

    def nxt(j):
        return pl.BlockSpec((None, 1, BAND_HALF, w),
                            lambda bi, r, i: (3 * g + j, bi, jnp.minimum((i + 1) * hb, last), r))

    kern = functools.partial(_band_kernel, tq=tq, seq=seq)
    o, lse = pl.pallas_call(
        kern,
        grid=(b, dil, seq // tq),
        in_specs=[cur(0), prev(1), cur(1), nxt(1), prev(2), cur(2), nxt(2)],
        out_specs=[pl.BlockSpec((1, tq, w), lambda bi, r, i: (bi, i, r)),
                   pl.BlockSpec((1, tq, w), lambda bi, r, i: (bi, i, r))],
        out_shape=[jax.ShapeDtypeStruct((b, seq, dil * w), _bf16),
                   jax.ShapeDtypeStruct((b, seq, dil * w), _f32)],
        compiler_params=pltpu.CompilerParams(vmem_limit_bytes=VMEM_LIMIT),
        name=f"band_attn_d{dil}",
    )(view, view, view, view, view, view, view)
    return o.reshape(b, s, w), lse.reshape(b, s, w)


def _out_kernel(x_ref, ot_ref, zm_ref, o0_ref, o1_ref, o2_ref, l0_ref, l1_ref, l2_ref,
                zd_ref, gt_ref, gate_ref, wpa_ref, wpb_ref, wo_ref, gf_ref, y_ref,
                *, final_norm):
    d = x_ref.shape[-1]
    o_mla = jnp.transpose(ot_ref[0].astype(_f32)) * zm_ref[0].astype(_f32)
    a = _dot(o_mla.astype(_bf16), wpa_ref[...])

    l0, l1, l2 = l0_ref[0], l1_ref[0], l2_ref[0]
    mx = jnp.maximum(jnp.maximum(l0, l1), l2)
    w0, w1, w2 = jnp.exp(l0 - mx), jnp.exp(l1 - mx), jnp.exp(l2 - mx)
    o_dil = (w0 * o0_ref[0].astype(_f32) + w1 * o1_ref[0].astype(_f32)
             + w2 * o2_ref[0].astype(_f32)) / (w0 + w1 + w2)
    o_dil = o_dil * zd_ref[0].astype(_f32)
    bb = _dot(o_dil.astype(_bf16), wpb_ref[...])

    gt = gt_ref[0].astype(_f32)
    u = gt[:, :d] * a + gt[:, d:] * bb
    y = x_ref[0] + gate_ref[0] * _dot(u.astype(_bf16), wo_ref[...])
    if final_norm:
        y = _rms(y, gf_ref[...])
    y_ref[0] = y


def _out_call(x, ot, zm, outs, lses, zd, gt, gate, lw, g_final, final_norm):
    b, s, d = x.shape
    tm = min(TOKEN_TILE, s)
    tok = lambda w: pl.BlockSpec((1, tm, w), lambda bi, ti: (bi, ti, 0))
    const = lambda *shape: pl.BlockSpec(shape, lambda bi, ti: (0,) * len(shape))
    kern = functools.partial(_out_kernel, final_norm=final_norm)
    return pl.pallas_call(
        kern,
        grid=(b, s // tm),
        in_specs=[tok(d),
                  pl.BlockSpec((1, MLA_WIDTH, tm), lambda bi, ti: (bi, 0, ti)),
                  tok(MLA_WIDTH),
                  tok(DIL_WIDTH), tok(DIL_WIDTH), tok(DIL_WIDTH),
                  tok(DIL_WIDTH), tok(DIL_WIDTH), tok(DIL_WIDTH),
                  tok(DIL_WIDTH), tok(2 * d),
                  pl.BlockSpec((1, 1, d), lambda bi, ti: (bi, 0, 0)),
                  const(MLA_WIDTH, d), const(DIL_WIDTH, d), const(d, d), const(1, d)],
        out_specs=tok(d),
        out_shape=jax.ShapeDtypeStruct((b, s, d), _f32),
        compiler_params=pltpu.CompilerParams(vmem_limit_bytes=VMEM_LIMIT),
        name="out_proj",
    )(x, ot.reshape(b, MLA_WIDTH, s), zm, outs[0], outs[1], outs[2], lses[0], lses[1], lses[2],
      zd, gt, gate, lw["w_pa"], lw["w_pb"], lw["w_out"], g_final)


def _mla_lane_order():
    half = MLA_ROPE // 2
    src = np.full((HEAD_PAD,), -1, np.int64)
    src[0:half] = MLA_NOPE + np.arange(half)
    src[half:64] = np.arange(64 - half)
    src[64:64 + half] = MLA_NOPE + half + np.arange(half)
    src[64 + half:64 + half + (MLA_NOPE - (64 - half))] = np.arange(64 - half, MLA_NOPE)
    return src


def _gather_cols(w, src):
    picked = jnp.take(w, jnp.asarray(np.maximum(src, 0)), axis=1)
    return jnp.where(jnp.asarray(src >= 0)[None, :], picked, 0.0)


def _dil_perm():
    idx = []
    half = DIL_HEAD_DIM // 2
    for p in range(DIL_HEADS // 2):
        ha, hb = 2 * p * DIL_HEAD_DIM, (2 * p + 1) * DIL_HEAD_DIM
        idx += list(range(ha, ha + half)) + list(range(hb, hb + half))
        idx += list(range(ha + half, ha + 2 * half)) + list(range(hb + half, hb + 2 * half))
    return np.asarray(idx, np.int64)


def _pack_layer(w_in, g_norm, b_gate, g_cq, w_uq, g_ckv, w_ukv, w_pa, w_pb, w_out):
    d = w_in.shape[0]
    splits = np.cumsum([Q_LORA, KV_LORA, MLA_ROPE, MLA_WIDTH] + [DIL_WIDTH] * (3 * N_GROUPS)
                       + [DIL_WIDTH])
    parts = jnp.split(w_in, list(splits), axis=1)
    perm = _dil_perm()
    cols = [parts[0], parts[1], jnp.pad(parts[2], ((0, 0), (0, KR_PAD - MLA_ROPE))), parts[3]]
    for j in range(3 * N_GROUPS):
        pj = parts[4 + j]
        cols.append(pj if j % 3 == 2 else jnp.take(pj, jnp.asarray(perm), axis=1))
    cols += [parts[4 + 3 * N_GROUPS], parts[5 + 3 * N_GROUPS]]
    w_big = jnp.concatenate(cols, axis=1).astype(_bf16)

    lane_src = _mla_lane_order()
    dq = MLA_NOPE + MLA_ROPE
    q_src = np.concatenate([np.where(lane_src >= 0, h * dq + lane_src, -1) for h in range(MLA_HEADS)])
    w_uq_p = _gather_cols(w_uq, q_src).astype(_bf16)
    dkv = MLA_NOPE + MLA_V
    nope_src = np.where((lane_src >= 0) & (lane_src < MLA_NOPE), lane_src, -1)
    k_src = np.concatenate([np.where(nope_src >= 0, h * dkv + nope_src, -1) for h in range(MLA_HEADS)])
    w_k = _gather_cols(w_ukv, k_src).astype(_bf16)
    v_src = np.concatenate([h * dkv + MLA_NOPE + np.arange(MLA_V) for h in range(MLA_HEADS)])
    w_vt = jnp.transpose(jnp.take(w_ukv, jnp.asarray(v_src), axis=1)).astype(_bf16)

    e = np.zeros((KR_PAD, MLA_HEADS * HEAD_PAD), np.float32)
    for lane, srcc in enumerate(lane_src):
        if srcc >= MLA_NOPE:
            for h in range(MLA_HEADS):
                e[srcc - MLA_NOPE, h * HEAD_PAD + lane] = 1.0
    return dict(
        w_big=w_big, g_norm=g_norm.reshape(1, d), b_gate=b_gate.reshape(1, -1),
        g_cq=g_cq.reshape(1, -1), w_uq=w_uq_p, g_ckv=g_ckv.reshape(1, -1), w_k=w_k,
        e_kr=jnp.asarray(e, _bf16), w_vt=w_vt,
        w_pa=w_pa.astype(_bf16), w_pb=w_pb.astype(_bf16), w_out=w_out.astype(_bf16))


def _rope_tables(s):
    pos = np.arange(s, dtype=np.float64)[:, None]
    lane = np.arange(LANES)

    def table(freq_idx, active, n_freq):
        inv = np.power(ROPE_THETA, -2.0 * freq_idx / (2 * n_freq))
        ang = pos * inv[None, :]
        cos = np.where(active[None, :], np.cos(ang), 1.0)
        sign = np.where(lane < 64, -1.0, 1.0)
        sin = np.where(active[None, :], np.sin(ang) * sign[None, :], 0.0)
        return np.stack([cos, sin]).astype(np.float32)

    half = MLA_ROPE // 2
    tab_m = table((lane % 64).clip(0, half - 1).astype(np.float64), (lane % 64) < half, half)
    tab_d = table((lane % 32).astype(np.float64), np.ones(LANES, bool), DIL_HEAD_DIM // 2)
    return jnp.asarray(tab_m), jnp.asarray(tab_d)


def _trunk(x, mods, layers, g_final):
    b, s, d = x.shape
    assert s % (DIL_GROUPS[-1][1] * BAND_HALF) == 0 and s % TOKEN_TILE == 0
    tab_m, tab_d = _rope_tables(s)
    depth = len(layers)
    for l, lw in enumerate(layers):
        mod = mods[l]
        shift, scale, gate = [mod[:, None, j * d:(j + 1) * d] for j in range(3)]
        q, k, vt, zm, dil, zd, gt = _in_call(x, scale, shift, lw, tab_m, tab_d)
        ot = _mla_call(q, k, vt)
        outs, lses = [], []
        for g, (_, dl) in enumerate(DIL_GROUPS):
            o_g, lse_g = _band_call(dil, g, dl)
            outs.append(o_g)
            lses.append(lse_g)
        x = _out_call(x, ot, zm, outs, lses, zd, gt, gate, lw, g_final.reshape(1, d),
                      final_norm=(l == depth - 1))
    return x


def kernel(x_prompt, x_sample, c_prompt, c_sample, w_ada, b_ada, g_norm, w_in, b_gate, g_cq, w_uq,
           g_ckv, w_ukv, w_pa, w_pb, w_out, g_final):
    depth = w_in.shape[0]
    bp, bs = c_prompt.shape[0], c_sample.shape[0]
    rows = -(-(bp + bs) // 8) * 8
    c_all = jnp.concatenate([c_prompt, c_sample], axis=0)
    c_all = jnp.pad(c_all, ((0, rows - bp - bs), (0, 0)))
    mods = _ada_call(c_all, w_ada, b_ada)
    layers = [_pack_layer(w_in[l], g_norm[l], b_gate[l], g_cq[l], w_uq[l], g_ckv[l], w_ukv[l],
                          w_pa[l], w_pb[l], w_out[l]) for l in range(depth)]
    y_prompt = _trunk(x_prompt, mods[:, :bp], layers, g_final)
    y_sample = _trunk(x_sample, mods[:, bp:bp + bs], layers, g_final)
    return (y_prompt, y_sample)
```

```python
import functools
import math

import numpy as np
import jax
import jax.numpy as jnp
from jax import lax
from jax.experimental import pallas as pl
from jax.experimental.pallas import tpu as pltpu

ROPE_THETA = 10000.0
EPS = 1e-6
NEG_INF = -1e30

MLA_HEADS = 8
MLA_NOPE = 64
MLA_ROPE = 32
MLA_V = 64
Q_LORA = 384
KV_LORA = 256
MLA_WIDTH = MLA_HEADS * MLA_V
DIL_GROUPS = ((128, 1), (512, 4), (2048, 16))
DIL_HEADS = 8
DIL_HEAD_DIM = 64
DIL_WIDTH = DIL_HEADS * DIL_HEAD_DIM
N_GROUPS = len(DIL_GROUPS)
BAND_HALF = 64

LANES = 128
KR_PAD = LANES
HEAD_PAD = LANES
DIL_SLABS = DIL_WIDTH // LANES

OFF_CQ = 0
OFF_CKV = OFF_CQ + Q_LORA
OFF_KR = OFF_CKV + KV_LORA
OFF_ZM = OFF_KR + KR_PAD
OFF_DIL = OFF_ZM + MLA_WIDTH
OFF_ZD = OFF_DIL + 3 * N_GROUPS * DIL_WIDTH
OFF_MG = OFF_ZD + DIL_WIDTH
W_BIG = OFF_MG + 2 * 1024

TOKEN_TILE = 512
MLA_Q_TILE = 1024
BAND_SUB = 128
BAND_ROWS = 512
VMEM_LIMIT = 56 * 1024 * 1024

_f32 = jnp.float32
_bf16 = jnp.bfloat16
_NT = (((1,), (1,)), ((), ()))


def _dot(a, b):
    return jnp.dot(a, b, preferred_element_type=_f32)


def _dot_nt(a, b):
    return lax.dot_general(a, b, _NT, preferred_element_type=_f32)


def _rms(x, g):
    return x * lax.rsqrt(jnp.mean(x * x, axis=-1, keepdims=True) + EPS) * g


def _sigmoid(x):
    return 1.0 / (1.0 + jnp.exp(-x))


def _rope_pairs(x, cos, sin):
    return x * cos + pltpu.roll(x, 64, axis=1) * sin


def _ada_kernel(c_ref, w_ref, b_ref, o_ref):
    c = c_ref[...]
    a = c * _sigmoid(c)
    a_hi = a.astype(_bf16)
    a_lo = (a - a_hi.astype(_f32)).astype(_bf16)
    w = w_ref[0]
    w_hi = w.astype(_bf16)
    w_lo = (w - w_hi.astype(_f32)).astype(_bf16)
    o_ref[0] = _dot(a_hi, w_hi) + _dot(a_hi, w_lo) + _dot(a_lo, w_hi) + b_ref[0]


def _ada_call(c_all, w_ada, b_ada):
    depth, d, n = w_ada.shape
    rows = c_all.shape[0]
    nb = n // d
    return pl.pallas_call(
        _ada_kernel,
        grid=(depth, nb),
        in_specs=[
            pl.BlockSpec((rows, d), lambda l, j: (0, 0)),
            pl.BlockSpec((1, d, d), lambda l, j: (l, 0, j)),
            pl.BlockSpec((1, 1, d), lambda l, j: (l, 0, j)),
        ],
        out_specs=pl.BlockSpec((1, rows, d), lambda l, j: (l, 0, j)),
        out_shape=jax.ShapeDtypeStruct((depth, rows, n), _f32),
        compiler_params=pltpu.CompilerParams(vmem_limit_bytes=VMEM_LIMIT),
        name="adaln",
    )(c_all, w_ada, b_ada.reshape(depth, 1, n))


def _in_kernel(x_ref, sc_ref, sh_ref, gn_ref, w_ref, gcq_ref, wuq_ref, gckv_ref, wk_ref,
               e_ref, wvt_ref, bg_ref, tm_ref, t1_ref, t4_ref, t16_ref,
               q_ref, k_ref, vt_ref, zm_ref, d1_ref, d4_ref, d16_ref, zd_ref, gt_ref,
               xs_ref, *, q_scale):
    x = x_ref[0]
    rows, d_model = x.shape

    def modulate(xv):
        return (_rms(xv, gn_ref[...]) * (1.0 + sc_ref[0]) + sh_ref[0]).astype(_bf16)

    hb = modulate(x)
    cos_m, sin_m = tm_ref[0], tm_ref[1]

    cq = _dot(hb, w_ref[:, OFF_CQ:OFF_CQ + Q_LORA])
    cqn = _rms(cq, gcq_ref[...]).astype(_bf16)
    q = _dot(cqn, wuq_ref[...])
    for hd in range(MLA_HEADS):
        sl = slice(hd * HEAD_PAD, (hd + 1) * HEAD_PAD)
        q_ref[0, hd] = (_rope_pairs(q[:, sl], cos_m, sin_m) * q_scale).astype(_bf16)

    ckv = _dot(hb, w_ref[:, OFF_CKV:OFF_CKV + KV_LORA])
    ckvn = _rms(ckv, gckv_ref[...]).astype(_bf16)
    kr = _dot(hb, w_ref[:, OFF_KR:OFF_KR + KR_PAD])
    kr_hi = kr.astype(_bf16)
    kr_lo = (kr - kr_hi.astype(_f32)).astype(_bf16)
    k = _dot(ckvn, wk_ref[...]) + _dot(kr_hi, e_ref[...]) + _dot(kr_lo, e_ref[...])
    for hd in range(MLA_HEADS):
        sl = slice(hd * HEAD_PAD, (hd + 1) * HEAD_PAD)
        k_ref[0, hd] = _rope_pairs(k[:, sl], cos_m, sin_m).astype(_bf16)
    vt = _dot_nt(wvt_ref[...], ckvn)
    for hd in range(MLA_HEADS):
        vt_ref[0, hd, 0] = vt[hd * MLA_V:(hd + 1) * MLA_V, :].astype(_bf16)

    zm = _dot(hb, w_ref[:, OFF_ZM:OFF_ZM + MLA_WIDTH])
    zm_ref[0] = (zm * _sigmoid(zm)).astype(_bf16)

    zd = _dot(hb, w_ref[:, OFF_ZD:OFF_ZD + DIL_WIDTH])
    zd_ref[0] = (zd * _sigmoid(zd)).astype(_bf16)

    mg = _dot(hb, w_ref[:, OFF_MG:W_BIG]) + bg_ref[...]
    gt_ref[0] = _sigmoid(mg).astype(_bf16)

    for c in range(d_model // LANES):
        xs_ref[c] = x[:, c * LANES:(c + 1) * LANES]
    for g, (dil, out_ref, tab_ref) in enumerate(((1, d1_ref, t1_ref), (4, d4_ref, t4_ref),
                                                 (16, d16_ref, t16_ref))):
        per = rows // dil
        if dil == 1:
            hg = hb
        else:
            xg = jnp.concatenate(
                [jnp.concatenate([xs_ref[c, pl.ds(r, per, stride=dil), :] for r in range(dil)], axis=0)
                 for c in range(d_model // LANES)], axis=1)
            hg = modulate(xg)
        cos_d, sin_d = tab_ref[0], tab_ref[1]
        for kind in range(3):
            off = OFF_DIL + (3 * g + kind) * DIL_WIDTH
            u = _dot(hg, w_ref[:, off:off + DIL_WIDTH])
            if kind == 2:
                ub = u.astype(_bf16)
            else:
                post = DIL_HEAD_DIM ** -0.5 if kind == 0 else 1.0
                ub = jnp.concatenate(
                    [(_rope_pairs(u[:, p * LANES:(p + 1) * LANES], cos_d, sin_d) * post).astype(_bf16)
                     for p in range(DIL_SLABS)], axis=1)
            for r in range(dil):
                out_ref[kind, 0, r] = ub[r * per:(r + 1) * per]


def _in_call(x, scale, shift, lw, tab_m, tabs_d):
    b, s, d = x.shape
    tm = min(TOKEN_TILE, s)
    nt = s // tm
    const = lambda *shape: pl.BlockSpec(shape, lambda bi, ti: (0,) * len(shape),
                                        pipeline_mode=pl.Buffered(1))
    tab = pl.BlockSpec((2, tm, LANES), lambda bi, ti: (0, ti, 0))
    in_specs = [
        pl.BlockSpec((1, tm, d), lambda bi, ti: (bi, ti, 0)),
        pl.BlockSpec((1, 1, d), lambda bi, ti: (bi, 0, 0)),
        pl.BlockSpec((1, 1, d), lambda bi, ti: (bi, 0, 0)),
        const(1, d),
        const(d, W_BIG),
        const(1, Q_LORA),
        const(Q_LORA, MLA_HEADS * HEAD_PAD),
        const(1, KV_LORA),
        const(KV_LORA, MLA_HEADS * HEAD_PAD),
        const(KR_PAD, MLA_HEADS * HEAD_PAD),
        const(MLA_WIDTH, KV_LORA),
        const(1, 2 * d),
        tab, tab, tab, tab,
    ]
    dil_shapes = [jax.ShapeDtypeStruct((3, b, dl, s // dl, DIL_WIDTH), _bf16) for _, dl in DIL_GROUPS]
    dil_specs = [pl.BlockSpec((3, 1, dl, tm // dl, DIL_WIDTH), lambda bi, ti: (0, bi, 0, ti, 0))
                 for _, dl in DIL_GROUPS]
    out_shape = [
        jax.ShapeDtypeStruct((b, MLA_HEADS, s, HEAD_PAD), _bf16),
        jax.ShapeDtypeStruct((b, MLA_HEADS, s, HEAD_PAD), _bf16),
        jax.ShapeDtypeStruct((b, MLA_HEADS, nt, MLA_V, tm), _bf16),
        jax.ShapeDtypeStruct((b, s, MLA_WIDTH), _bf16),
        *dil_shapes,
        jax.ShapeDtypeStruct((b, s, DIL_WIDTH), _bf16),
        jax.ShapeDtypeStruct((b, s, 2 * d), _bf16),
    ]
    out_specs = [
        pl.BlockSpec((1, MLA_HEADS, tm, HEAD_PAD), lambda bi, ti: (bi, 0, ti, 0)),
        pl.BlockSpec((1, MLA_HEADS, tm, HEAD_PAD), lambda bi, ti: (bi, 0, ti, 0)),
        pl.BlockSpec((1, MLA_HEADS, 1, MLA_V, tm), lambda bi, ti: (bi, 0, ti, 0, 0)),
        pl.BlockSpec((1, tm, MLA_WIDTH), lambda bi, ti: (bi, ti, 0)),
        *dil_specs,
        pl.BlockSpec((1, tm, DIL_WIDTH), lambda bi, ti: (bi, ti, 0)),
        pl.BlockSpec((1, tm, 2 * d), lambda bi, ti: (bi, ti, 0)),
    ]
    kern = functools.partial(_in_kernel, q_scale=(MLA_NOPE + MLA_ROPE) ** -0.5 * math.log2(math.e))
    return pl.pallas_call(
        kern,
        grid=(b, nt),
        in_specs=in_specs,
        out_specs=out_specs,
        out_shape=out_shape,
        scratch_shapes=[pltpu.VMEM((d // LANES, tm, LANES), _f32)],
        compiler_params=pltpu.CompilerParams(vmem_limit_bytes=VMEM_LIMIT),
        name="in_proj",
    )(x, scale, shift, lw["g_norm"], lw["w_big"], lw["g_cq"], lw["w_uq"], lw["g_ckv"],
      lw["w_k"], lw["e_kr"], lw["w_vt"], lw["b_gate"], tab_m, *tabs_d)


def _mla_kernel(q_ref, k_ref, vt_ref, o_ref, sa_ref, sb_ref, m_ref, acc_ref, *, n_chunks, tk):
    q = q_ref[0, 0]
    m_ref[...] = jnp.full(m_ref.shape, NEG_INF, _f32)
    acc_ref[...] = jnp.zeros(acc_ref.shape, _f32)
    ones = jnp.ones((16, tk), _bf16)

    def scores(c, dst_ref):
        start = pl.multiple_of(c * tk, tk)
        st = _dot_nt(k_ref[0, 0, pl.ds(start, tk), :], q)
        dst_ref[...] = st
        return jnp.max(st, axis=0, keepdims=True)

    def consume(c, src_ref, cmax):
        m_old = m_ref[...]
        m_new = jnp.maximum(m_old, cmax)
        alpha = jnp.exp2(m_old - m_new)
        p = jnp.exp2(src_ref[...] - m_new).astype(_bf16)
        va = jnp.concatenate([vt_ref[0, 0, c], ones], axis=0)
        acc_ref[...] = alpha * acc_ref[...] + _dot(va, p)
        m_ref[...] = m_new

    def body(j, cm_a):
        cm_b = scores(2 * j + 1, sb_ref)
        consume(2 * j, sa_ref, cm_a)
        cm_next = scores(2 * j + 2, sa_ref)
        consume(2 * j + 1, sb_ref, cm_b)
        return cm_next

    cm_a = lax.fori_loop(0, n_chunks // 2 - 1, body, scores(0, sa_ref))
    cm_b = scores(n_chunks - 1, sb_ref)
    consume(n_chunks - 2, sa_ref, cm_a)
    consume(n_chunks - 1, sb_ref, cm_b)
    acc = acc_ref[...]
    o_ref[0, 0] = (acc[:MLA_V] / acc[MLA_V:MLA_V + 1]).astype(_bf16)


def _mla_call(q, k, vt):
    b, hh, s, _ = q.shape
    n_chunks, tk = vt.shape[2], vt.shape[4]
    assert n_chunks % 2 == 0
    tq = min(MLA_Q_TILE, s)
    kern = functools.partial(_mla_kernel, n_chunks=n_chunks, tk=tk)
    return pl.pallas_call(
        kern,
        grid=(b, hh, s // tq),
        in_specs=[
            pl.BlockSpec((1, 1, tq, HEAD_PAD), lambda bi, hi, qi: (bi, hi, qi, 0)),
            pl.BlockSpec((1, 1, s, HEAD_PAD), lambda bi, hi, qi: (bi, hi, 0, 0)),
            pl.BlockSpec((1, 1, n_chunks, MLA_V, tk), lambda bi, hi, qi: (bi, hi, 0, 0, 0)),
        ],
        out_specs=pl.BlockSpec((1, 1, MLA_V, tq), lambda bi, hi, qi: (bi, hi, 0, qi)),
        out_shape=jax.ShapeDtypeStruct((b, hh, MLA_V, s), _bf16),
        scratch_shapes=[pltpu.VMEM((tk, tq), _f32), pltpu.VMEM((tk, tq), _f32),
                        pltpu.VMEM((1, tq), _f32), pltpu.VMEM((MLA_V + 16, tq), _f32)],
        compiler_params=pltpu.CompilerParams(vmem_limit_bytes=VMEM_LIMIT),
        name="mla_attn",
    )(q, k, vt)


def _band_kernel(q_ref, kp_ref, kc_ref, kn_ref, vp_ref, vc_ref, vn_ref, o_ref, lse_ref,
                 *, dil, tq, seq):
    i = pl.program_id(1)
    nk = BAND_SUB + 2 * BAND_HALF
    row = lax.broadcasted_iota(jnp.int32, (BAND_SUB, nk), 0)
    col = lax.broadcasted_iota(jnp.int32, (BAND_SUB, nk), 1)
    band = jnp.abs(col - BAND_HALF - row) <= BAND_HALF
    lane = lax.broadcasted_iota(jnp.int32, (1, LANES), 1)
    first_head = (lane % 64) < 32
    low_half = lane < 64

    def residue(r, carry):
        q = q_ref[0, 0, r]
        kk = jnp.concatenate([kp_ref[0, 0, r], kc_ref[0, 0, r], kn_ref[0, 0, r]], axis=0)
        vv = jnp.concatenate([vp_ref[0, 0, r], vc_ref[0, 0, r], vn_ref[0, 0, r]], axis=0)
        for sb in range(tq // BAND_SUB):
            a = sb * BAND_SUB
            kidx = i * tq + (a - BAND_HALF) + col
            valid = band & (kidx >= 0) & (kidx < seq)
            for p in range(DIL_SLABS):
                sl = slice(p * LANES, (p + 1) * LANES)
                qp = q[a:a + BAND_SUB, sl]
                kp = kk[a:a + nk, sl]
                vp = vv[a:a + nk, sl]
                res = []
                for sel in (first_head, jnp.logical_not(first_head)):
                    qm = jnp.where(sel, qp, jnp.zeros_like(qp))
                    sc = jnp.where(valid, _dot_nt(qm, kp), NEG_INF)
                    m = jnp.max(sc, axis=-1, keepdims=True)
                    e = jnp.exp(sc - m)
                    l = jnp.sum(e, axis=-1, keepdims=True)
                    o = _dot(e.astype(_bf16), vp) / l
                    res.append((o, m + jnp.log(l)))
                dst = pl.ds(r + a * dil, BAND_SUB, stride=dil) if dil > 1 else pl.ds(a, BAND_SUB)
                o_ref[0, p, dst, :] = jnp.where(low_half, res[0][0], res[1][0])
                lse_ref[0, p, dst, :] = jnp.where(low_half, res[0][1], res[1][1])
        return carry

    lax.fori_loop(0, dil, residue, 0)


def _band_call(qkv, dil):
    _, b, _, seq, w = qkv.shape
    s = seq * dil
    tq = max(BAND_SUB, BAND_ROWS // dil)
    assert tq % BAND_SUB == 0 and seq % tq == 0
    hb = tq // BAND_HALF
    last = seq // BAND_HALF - 1

    def cur(j):
        return pl.BlockSpec((1, 1, dil, tq, w), lambda bi, i: (j, bi, 0, i, 0))

    def prev(j):
        return pl.BlockSpec((1, 1, dil, BAND_HALF, w),
                            lambda bi, i: (j, bi, 0, jnp.maximum(i * hb - 1, 0), 0))

    def nxt(j):
        return pl.BlockSpec((1, 1, dil, BAND_HALF, w),
                            lambda bi, i: (j, bi, 0, jnp.minimum((i + 1) * hb, last), 0))

    out_spec = pl.BlockSpec((1, DIL_SLABS, dil * tq, LANES), lambda bi, i: (bi, 0, i, 0))
    out_sds = jax.ShapeDtypeStruct((b, DIL_SLABS, s, LANES), _f32)
    kern = functools.partial(_band_kernel, dil=dil, tq=tq, seq=seq)
    return pl.pallas_call(
        kern,
        grid=(b, seq // tq),
        in_specs=[cur(0), prev(1), cur(1), nxt(1), prev(2), cur(2), nxt(2)],
        out_specs=[out_spec, out_spec],
        out_shape=[out_sds, out_sds],
        compiler_params=pltpu.CompilerParams(vmem_limit_bytes=VMEM_LIMIT),
        name=f"band_attn_d{dil}",
    )(qkv, qkv, qkv, qkv, qkv, qkv, qkv)


def _out_kernel(x_ref, ot_ref, zm_ref, o0_ref, o1_ref, o2_ref, l0_ref, l1_ref, l2_ref,
                zd_ref, gt_ref, gate_ref, wpa_ref, wpb_ref, wo_ref, gf_ref, y_ref,
                *, final_norm):
    d = x_ref.shape[-1]
    o_mla = jnp.transpose(ot_ref[0].astype(_f32)) * zm_ref[0].astype(_f32)
    a = _dot(o_mla.astype(_bf16), wpa_ref[...])

    slabs = []
    for p in range(DIL_SLABS):
        l0, l1, l2 = l0_ref[0, p], l1_ref[0, p], l2_ref[0, p]
        mx = jnp.maximum(jnp.maximum(l0, l1), l2)
        w0, w1, w2 = jnp.exp(l0 - mx), jnp.exp(l1 - mx), jnp.exp(l2 - mx)
        slabs.append((w0 * o0_ref[0, p] + w1 * o1_ref[0, p] + w2 * o2_ref[0, p]) / (w0 + w1 + w2))
    o_dil = jnp.concatenate(slabs, axis=1) * zd_ref[0].astype(_f32)
    bb = _dot(o_dil.astype(_bf16), wpb_ref[...])

    gt = gt_ref[0].astype(_f32)
    u = gt[:, :d] * a + gt[:, d:] * bb
    y = x_ref[0] + gate_ref[0] * _dot(u.astype(_bf16), wo_ref[...])
    if final_norm:
        y = _rms(y, gf_ref[...])
    y_ref[0] = y


def _out_call(x, ot, zm, outs, lses, zd, gt, gate, lw, g_final, final_norm):
    b, s, d = x.shape
    tm = min(TOKEN_TILE, s)
    tok = lambda w: pl.BlockSpec((1, tm, w), lambda bi, ti: (bi, ti, 0))
    slab = pl.BlockSpec((1, DIL_SLABS, tm, LANES), lambda bi, ti: (bi, 0, ti, 0))
    const = lambda *shape: pl.BlockSpec(shape, lambda bi, ti: (0,) * len(shape))
    kern = functools.partial(_out_kernel, final_norm=final_norm)
    return pl.pallas_call(
        kern,
        grid=(b, s // tm),
        in_specs=[tok(d),
                  pl.BlockSpec((1, MLA_WIDTH, tm), lambda bi, ti: (bi, 0, ti)),
                  tok(MLA_WIDTH),
                  slab, slab, slab, slab, slab, slab,
                  tok(DIL_WIDTH), tok(2 * d),
                  pl.BlockSpec((1, 1, d), lambda bi, ti: (bi, 0, 0)),
                  const(MLA_WIDTH, d), const(DIL_WIDTH, d), const(d, d), const(1, d)],
        out_specs=tok(d),
        out_shape=jax.ShapeDtypeStruct((b, s, d), _f32),
        compiler_params=pltpu.CompilerParams(vmem_limit_bytes=VMEM_LIMIT),
        name="out_proj",
    )(x, ot.reshape(b, MLA_WIDTH, s), zm, outs[0], outs[1], outs[2], lses[0], lses[1], lses[2],
      zd, gt, gate, lw["w_pa"], lw["w_pb"], lw["w_out"], g_final)


def _mla_lane_order():
    half = MLA_ROPE // 2
    src = np.full((HEAD_PAD,), -1, np.int64)
    src[0:half] = MLA_NOPE + np.arange(half)
    src[half:64] = np.arange(64 - half)
    src[64:64 + half] = MLA_NOPE + half + np.arange(half)
    src[64 + half:64 + half + (MLA_NOPE - (64 - half))] = np.arange(64 - half, MLA_NOPE)
    return src


def _gather_cols(w, src):
    picked = jnp.take(w, jnp.asarray(np.maximum(src, 0)), axis=1)
    return jnp.where(jnp.asarray(src >= 0)[None, :], picked, 0.0)


def _dil_perm():
    idx = []
    half = DIL_HEAD_DIM // 2
    for p in range(DIL_HEADS // 2):
        ha, hb = 2 * p * DIL_HEAD_DIM, (2 * p + 1) * DIL_HEAD_DIM
        idx += list(range(ha, ha + half)) + list(range(hb, hb + half))
        idx += list(range(ha + half, ha + 2 * half)) + list(range(hb + half, hb + 2 * half))
    return np.asarray(idx, np.int64)


def _pack_layer(w_in, g_norm, b_gate, g_cq, w_uq, g_ckv, w_ukv, w_pa, w_pb, w_out):
    d = w_in.shape[0]
    splits = np.cumsum([Q_LORA, KV_LORA, MLA_ROPE, MLA_WIDTH] + [DIL_WIDTH] * (3 * N_GROUPS)
                       + [DIL_WIDTH])
    parts = jnp.split(w_in, list(splits), axis=1)
    perm = _dil_perm()
    cols = [parts[0], parts[1], jnp.pad(parts[2], ((0, 0), (0, KR_PAD - MLA_ROPE))), parts[3]]
    for j in range(3 * N_GROUPS):
        pj = parts[4 + j]
        cols.append(pj if j % 3 == 2 else jnp.take(pj, jnp.asarray(perm), axis=1))
    cols += [parts[4 + 3 * N_GROUPS], parts[5 + 3 * N_GROUPS]]
    w_big = jnp.concatenate(cols, axis=1).astype(_bf16)

    lane_src = _mla_lane_order()
    dq = MLA_NOPE + MLA_ROPE
    q_src = np.concatenate([np.where(lane_src >= 0, h * dq + lane_src, -1) for h in range(MLA_HEADS)])
    w_uq_p = _gather_cols(w_uq, q_src).astype(_bf16)
    dkv = MLA_NOPE + MLA_V
    nope_src = np.where((lane_src >= 0) & (lane_src < MLA_NOPE), lane_src, -1)
    k_src = np.concatenate([np.where(nope_src >= 0, h * dkv + nope_src, -1) for h in range(MLA_HEADS)])
    w_k = _gather_cols(w_ukv, k_src).astype(_bf16)
    v_src = np.concatenate([h * dkv + MLA_NOPE + np.arange(MLA_V) for h in range(MLA_HEADS)])
    w_vt = jnp.transpose(jnp.take(w_ukv, jnp.asarray(v_src), axis=1)).astype(_bf16)

    e = np.zeros((KR_PAD, MLA_HEADS * HEAD_PAD), np.float32)
    for lane, srcc in enumerate(lane_src):
        if srcc >= MLA_NOPE:
            for h in range(MLA_HEADS):
                e[srcc - MLA_NOPE, h * HEAD_PAD + lane] = 1.0
    return dict(
        w_big=w_big, g_norm=g_norm.reshape(1, d), b_gate=b_gate.reshape(1, -1),
        g_cq=g_cq.reshape(1, -1), w_uq=w_uq_p, g_ckv=g_ckv.reshape(1, -1), w_k=w_k,
        e_kr=jnp.asarray(e, _bf16), w_vt=w_vt,
        w_pa=w_pa.astype(_bf16), w_pb=w_pb.astype(_bf16), w_out=w_out.astype(_bf16))


def _rope_tables(s, tm):
    pos = np.arange(s, dtype=np.float64)[:, None]
    lane = np.arange(LANES)

    def table(freq_idx, active, n_freq):
        inv = np.power(ROPE_THETA, -2.0 * freq_idx / (2 * n_freq))
        ang = pos * inv[None, :]
        cos = np.where(active[None, :], np.cos(ang), 1.0)
        sign = np.where(lane < 64, -1.0, 1.0)
        sin = np.where(active[None, :], np.sin(ang) * sign[None, :], 0.0)
        return np.stack([cos, sin]).astype(np.float32)

    half = MLA_ROPE // 2
    tab_m = table((lane % 64).clip(0, half - 1).astype(np.float64), (lane % 64) < half, half)
    tab_d = jnp.asarray(table((lane % 32).astype(np.float64), np.ones(LANES, bool), DIL_HEAD_DIM // 2))
    tabs = []
    for _, dil in DIL_GROUPS:
        t = tab_d.reshape(2, s // tm, tm // dil, dil, LANES)
        tabs.append(jnp.swapaxes(t, 2, 3).reshape(2, s, LANES))
    return jnp.asarray(tab_m), tabs


def _trunk(x, mods, layers, g_final):
    b, s, d = x.shape
    assert s % (DIL_GROUPS[-1][1] * BAND_SUB) == 0 and s % TOKEN_TILE == 0
    tab_m, tabs_d = _rope_tables(s, min(TOKEN_TILE, s))
    depth = len(layers)
    for l, lw in enumerate(layers):
        mod = mods[l]
        shift, scale, gate = [mod[:, None, j * d:(j + 1) * d] for j in range(3)]
        q, k, vt, zm, d1, d4, d16, zd, gt = _in_call(x, scale, shift, lw, tab_m, tabs_d)
        ot = _mla_call(q, k, vt)
        outs, lses = [], []
        for qkv, (_, dl) in zip((d1, d4, d16), DIL_GROUPS):
            o_g, lse_g = _band_call(qkv, dl)
            outs.append(o_g)
            lses.append(lse_g)
        x = _out_call(x, ot, zm, outs, lses, zd, gt, gate, lw, g_final.reshape(1, d),
                      final_norm=(l == depth - 1))
    return x


def kernel(x_prompt, x_sample, c_prompt, c_sample, w_ada, b_ada, g_norm, w_in, b_gate, g_cq, w_uq,
           g_ckv, w_ukv, w_pa, w_pb, w_out, g_final):
    depth = w_in.shape[0]
    bp, bs = c_prompt.shape[0], c_sample.shape[0]
    rows = -(-(bp + bs) // 8) * 8
    c_all = jnp.concatenate([c_prompt, c_sample], axis=0)
    c_all = jnp.pad(c_all, ((0, rows - bp - bs), (0, 0)))
    mods = _ada_call(c_all, w_ada, b_ada)
    layers = [_pack_layer(w_in[l], g_norm[l], b_gate[l], g_cq[l], w_uq[l], g_ckv[l], w_ukv[l],
                          w_pa[l], w_pb[l], w_out[l]) for l in range(depth)]
    y_prompt = _trunk(x_prompt, mods[:, :bp], layers, g_final)
    y_sample = _trunk(x_sample, mods[:, bp:bp + bs], layers, g_final)
    return (y_prompt, y_sample)
```

```python
import functools
import math

import numpy as np
import jax
import jax.numpy as jnp
from jax import lax
from jax.experimental import pallas as pl
from jax.experimental.pallas import tpu as pltpu

ROPE_THETA = 10000.0
EPS = 1e-6
NEG_INF = -1e30

MLA_HEADS = 8
MLA_NOPE = 64
MLA_ROPE = 32
MLA_V = 64
Q_LORA = 384
KV_LORA = 256
MLA_WIDTH = MLA_HEADS * MLA_V
DIL_GROUPS = ((128, 1), (512, 4), (2048, 16))
DIL_HEADS = 8
DIL_HEAD_DIM = 64
DIL_WIDTH = DIL_HEADS * DIL_HEAD_DIM
N_GROUPS = len(DIL_GROUPS)
BAND_HALF = 64

LANES = 128
KR_PAD = LANES
HEAD_PAD = LANES
DIL_SLABS = DIL_WIDTH // LANES

OFF_CQ = 0
OFF_CKV = OFF_CQ + Q_LORA
OFF_KR = OFF_CKV + KV_LORA
OFF_ZM = OFF_KR + KR_PAD
OFF_DIL = OFF_ZM + MLA_WIDTH
OFF_ZD = OFF_DIL + 3 * N_GROUPS * DIL_WIDTH
OFF_MG = OFF_ZD + DIL_WIDTH
W_BIG = OFF_MG + 2 * 1024

TOKEN_TILE = 512
MLA_Q_TILE = 1024
MLA_K_CHUNK = 512
MLA_UNROLL = 4
BAND_SUB = 128
BAND_ROWS = 512
VMEM_LIMIT = 56 * 1024 * 1024

_f32 = jnp.float32
_bf16 = jnp.bfloat16
_NT = (((1,), (1,)), ((), ()))


def _dot(a, b):
    return jnp.dot(a, b, preferred_element_type=_f32)


def _dot_nt(a, b):
    return lax.dot_general(a, b, _NT, preferred_element_type=_f32)


def _rms(x, g):
    return x * lax.rsqrt(jnp.mean(x * x, axis=-1, keepdims=True) + EPS) * g


def _sigmoid(x):
    return 1.0 / (1.0 + jnp.exp(-x))


def _rope_pairs(x, cos, sin):
    return x * cos + pltpu.roll(x, 64, axis=1) * sin


def _ada_kernel(c_ref, w_ref, b_ref, o_ref):
    c = c_ref[...]
    a = c * _sigmoid(c)
    a_hi = a.astype(_bf16)
    a_lo = (a - a_hi.astype(_f32)).astype(_bf16)
    w = w_ref[0]
    w_hi = w.astype(_bf16)
    w_lo = (w - w_hi.astype(_f32)).astype(_bf16)
    o_ref[0] = _dot(a_hi, w_hi) + _dot(a_hi, w_lo) + _dot(a_lo, w_hi) + b_ref[0]


def _ada_call(c_all, w_ada, b_ada):
    depth, d, n = w_ada.shape
    rows = c_all.shape[0]
    nb = n // d
    return pl.pallas_call(
        _ada_kernel,
        grid=(depth, nb),
        in_specs=[
            pl.BlockSpec((rows, d), lambda l, j: (0, 0)),
            pl.BlockSpec((1, d, d), lambda l, j: (l, 0, j)),
            pl.BlockSpec((1, 1, d), lambda l, j: (l, 0, j)),
        ],
        out_specs=pl.BlockSpec((1, rows, d), lambda l, j: (l, 0, j)),
        out_shape=jax.ShapeDtypeStruct((depth, rows, n), _f32),
        compiler_params=pltpu.CompilerParams(vmem_limit_bytes=VMEM_LIMIT),
        name="adaln",
    )(c_all, w_ada, b_ada.reshape(depth, 1, n))


def _in_kernel(x_ref, sc_ref, sh_ref, gn_ref, w_ref, gcq_ref, wuq_ref, gckv_ref, wk_ref,
               wvt_ref, bg_ref, tm_ref, t1_ref, t4_ref, t16_ref,
               q_ref, k_ref, vt_ref, zm_ref, d1_ref, d4_ref, d16_ref, zd_ref, gt_ref,
               xs_ref, *, q_scale):
    x = x_ref[0]
    rows, d_model = x.shape

    def modulate(xv):
        return (_rms(xv, gn_ref[...]) * (1.0 + sc_ref[0]) + sh_ref[0]).astype(_bf16)

    hb = modulate(x)
    cos_m, sin_m = tm_ref[0], tm_ref[1]

    cq = _dot(hb, w_ref[:, OFF_CQ:OFF_CQ + Q_LORA])
    cqn = _rms(cq, gcq_ref[...]).astype(_bf16)
    q = _dot(cqn, wuq_ref[...])
    for hd in range(MLA_HEADS):
        sl = slice(hd * HEAD_PAD, (hd + 1) * HEAD_PAD)
        q_ref[0, hd] = (_rope_pairs(q[:, sl], cos_m, sin_m) * q_scale).astype(_bf16)

    ckv = _dot(hb, w_ref[:, OFF_CKV:OFF_CKV + KV_LORA])
    ckvn = _rms(ckv, gckv_ref[...]).astype(_bf16)
    kr = _dot(hb, w_ref[:, OFF_KR:OFF_KR + KR_PAD])
    k = _dot(ckvn, wk_ref[...])
    for hd in range(MLA_HEADS):
        sl = slice(hd * HEAD_PAD, (hd + 1) * HEAD_PAD)
        k_ref[0, hd] = _rope_pairs(k[:, sl] + kr, cos_m, sin_m).astype(_bf16)
    vt = _dot_nt(wvt_ref[...], ckvn)
    for hd in range(MLA_HEADS):
        vt_ref[0, hd, 0] = vt[hd * MLA_V:(hd + 1) * MLA_V, :].astype(_bf16)

    zm = _dot(hb, w_ref[:, OFF_ZM:OFF_ZM + MLA_WIDTH])
    zm_ref[0] = (zm * _sigmoid(zm)).astype(_bf16)

    zd = _dot(hb, w_ref[:, OFF_ZD:OFF_ZD + DIL_WIDTH])
    zd_ref[0] = (zd * _sigmoid(zd)).astype(_bf16)

    mg = _dot(hb, w_ref[:, OFF_MG:W_BIG]) + bg_ref[...]
    gt_ref[0] = _sigmoid(mg).astype(_bf16)

    for c in range(d_model // LANES):
        xs_ref[c] = x[:, c * LANES:(c + 1) * LANES]
    for g, (dil, out_ref, tab_ref) in enumerate(((1, d1_ref, t1_ref), (4, d4_ref, t4_ref),
                                                 (16, d16_ref, t16_ref))):
        per = rows // dil
        if dil == 1:
            hg = hb
        else:
            xg = jnp.concatenate(
                [jnp.concatenate([xs_ref[c, pl.ds(r, per, stride=dil), :] for r in range(dil)], axis=0)
                 for c in range(d_model // LANES)], axis=1)
            hg = modulate(xg)
        cos_d, sin_d = tab_ref[0], tab_ref[1]
        for kind in range(3):
            off = OFF_DIL + (3 * g + kind) * DIL_WIDTH
            u = _dot(hg, w_ref[:, off:off + DIL_WIDTH])
            if kind == 2:
                ub = u.astype(_bf16)
            else:
                post = DIL_HEAD_DIM ** -0.5 if kind == 0 else 1.0
                ub = jnp.concatenate(
                    [(_rope_pairs(u[:, p * LANES:(p + 1) * LANES], cos_d, sin_d) * post).astype(_bf16)
                     for p in range(DIL_SLABS)], axis=1)
            for r in range(dil):
                out_ref[kind, 0, r] = ub[r * per:(r + 1) * per]


def _in_call(x, scale, shift, lw, tab_m, tabs_d):
    b, s, d = x.shape
    tm = min(TOKEN_TILE, s)
    nt = s // tm
    const = lambda *shape: pl.BlockSpec(shape, lambda bi, ti: (0,) * len(shape),
                                        pipeline_mode=pl.Buffered(1))
    tab = pl.BlockSpec((2, tm, LANES), lambda bi, ti: (0, ti, 0))
    in_specs = [
        pl.BlockSpec((1, tm, d), lambda bi, ti: (bi, ti, 0)),
        pl.BlockSpec((1, 1, d), lambda bi, ti: (bi, 0, 0)),
        pl.BlockSpec((1, 1, d), lambda bi, ti: (bi, 0, 0)),
        const(1, d),
        const(d, W_BIG),
        const(1, Q_LORA),
        const(Q_LORA, MLA_HEADS * HEAD_PAD),
        const(1, KV_LORA),
        const(KV_LORA, MLA_HEADS * HEAD_PAD),
        const(MLA_WIDTH, KV_LORA),
        const(1, 2 * d),
        tab, tab, tab, tab,
    ]
    dil_shapes = [jax.ShapeDtypeStruct((3, b, dl, s // dl, DIL_WIDTH), _bf16) for _, dl in DIL_GROUPS]
    dil_specs = [pl.BlockSpec((3, 1, dl, tm // dl, DIL_WIDTH), lambda bi, ti: (0, bi, 0, ti, 0))
                 for _, dl in DIL_GROUPS]
    out_shape = [
        jax.ShapeDtypeStruct((b, MLA_HEADS, s, HEAD_PAD), _bf16),
        jax.ShapeDtypeStruct((b, MLA_HEADS, s, HEAD_PAD), _bf16),
        jax.ShapeDtypeStruct((b, MLA_HEADS, nt, MLA_V, tm), _bf16),
        jax.ShapeDtypeStruct((b, s, MLA_WIDTH), _bf16),
        *dil_shapes,
        jax.ShapeDtypeStruct((b, s, DIL_WIDTH), _bf16),
        jax.ShapeDtypeStruct((b, s, 2 * d), _bf16),
    ]
    out_specs = [
        pl.BlockSpec((1, MLA_HEADS, tm, HEAD_PAD), lambda bi, ti: (bi, 0, ti, 0)),
        pl.BlockSpec((1, MLA_HEADS, tm, HEAD_PAD), lambda bi, ti: (bi, 0, ti, 0)),
        pl.BlockSpec((1, MLA_HEADS, 1, MLA_V, tm), lambda bi, ti: (bi, 0, ti, 0, 0)),
        pl.BlockSpec((1, tm, MLA_WIDTH), lambda bi, ti: (bi, ti, 0)),
        *dil_specs,
        pl.BlockSpec((1, tm, DIL_WIDTH), lambda bi, ti: (bi, ti, 0)),
        pl.BlockSpec((1, tm, 2 * d), lambda bi, ti: (bi, ti, 0)),
    ]
    kern = functools.partial(_in_kernel, q_scale=(MLA_NOPE + MLA_ROPE) ** -0.5 * math.log2(math.e))
    return pl.pallas_call(
        kern,
        grid=(b, nt),
        in_specs=in_specs,
        out_specs=out_specs,
        out_shape=out_shape,
        scratch_shapes=[pltpu.VMEM((d // LANES, tm, LANES), _f32)],
        compiler_params=pltpu.CompilerParams(vmem_limit_bytes=VMEM_LIMIT),
        name="in_proj",
    )(x, scale, shift, lw["g_norm"], lw["w_big"], lw["g_cq"], lw["w_uq"], lw["g_ckv"],
      lw["w_k"], lw["w_vt"], lw["b_gate"], tab_m, *tabs_d)


def _mla_kernel(q_ref, k_ref, vt_ref, o_ref, sa_ref, sb_ref, m_ref, acc_ref, *, n_chunks, tk,
                unroll):
    q = q_ref[0, 0]
    m_ref[...] = jnp.full(m_ref.shape, NEG_INF, _f32)
    acc_ref[...] = jnp.zeros(acc_ref.shape, _f32)
    ones = jnp.ones((16, tk), _bf16)
    per = tk // vt_ref.shape[-1]

    def scores(c, dst_ref):
        start = pl.multiple_of(c * tk, tk)
        st = _dot_nt(k_ref[0, 0, pl.ds(start, tk), :], q)
        dst_ref[...] = st
        return jnp.max(st, axis=0, keepdims=True)

    def consume(c, src_ref, cmax):
        m_old = m_ref[...]
        m_new = jnp.maximum(m_old, cmax)
        alpha = jnp.exp2(m_old - m_new)
        p = jnp.exp2(src_ref[...] - m_new).astype(_bf16)
        vc = jnp.concatenate([vt_ref[0, 0, c * per + j] for j in range(per)], axis=1)
        va = jnp.concatenate([vc, ones], axis=0)
        acc_ref[...] = alpha * acc_ref[...] + _dot(va, p)
        m_ref[...] = m_new

    bufs = (sa_ref, sb_ref)

    def steps(first, count, cm):
        for u in range(count):
            cm_next = scores(first + u + 1, bufs[(u + 1) % 2])
            consume(first + u, bufs[u % 2], cm)
            cm = cm_next
        return cm

    cm = lax.fori_loop(0, n_chunks // unroll - 1,
                       lambda j, cm: steps(j * unroll, unroll, cm), scores(0, sa_ref))
    cm = steps(n_chunks - unroll, unroll - 1, cm)
    consume(n_chunks - 1, bufs[(unroll - 1) % 2], cm)
    acc = acc_ref[...]
    o_ref[0, 0] = (acc[:MLA_V] / acc[MLA_V:MLA_V + 1]).astype(_bf16)


def _mla_call(q, k, vt):
    b, hh, s, _ = q.shape
    nv, tv = vt.shape[2], vt.shape[4]
    tk = min(MLA_K_CHUNK, s // 2)
    n_chunks = s // tk
    unroll = MLA_UNROLL if n_chunks >= 4 * MLA_UNROLL else 2
    assert n_chunks % unroll == 0 and unroll % 2 == 0 and tk % tv == 0
    tq = min(MLA_Q_TILE, s)
    kern = functools.partial(_mla_kernel, n_chunks=n_chunks, tk=tk, unroll=unroll)
    return pl.pallas_call(
        kern,
        grid=(b, hh, s // tq),
        in_specs=[
            pl.BlockSpec((1, 1, tq, HEAD_PAD), lambda bi, hi, qi: (bi, hi, qi, 0)),
            pl.BlockSpec((1, 1, s, HEAD_PAD), lambda bi, hi, qi: (bi, hi, 0, 0)),
            pl.BlockSpec((1, 1, nv, MLA_V, tv), lambda bi, hi, qi: (bi, hi, 0, 0, 0)),
        ],
        out_specs=pl.BlockSpec((1, 1, MLA_V, tq), lambda bi, hi, qi: (bi, hi, 0, qi)),
        out_shape=jax.ShapeDtypeStruct((b, hh, MLA_V, s), _bf16),
        scratch_shapes=[pltpu.VMEM((tk, tq), _f32), pltpu.VMEM((tk, tq), _f32),
                        pltpu.VMEM((1, tq), _f32), pltpu.VMEM((MLA_V + 16, tq), _f32)],
        compiler_params=pltpu.CompilerParams(vmem_limit_bytes=VMEM_LIMIT),
        name="mla_attn",
    )(q, k, vt)


def _band_kernel(q_ref, kp_ref, kc_ref, kn_ref, vp_ref, vc_ref, vn_ref, o_ref, lse_ref,
                 *, dil, tq, seq):
    i = pl.program_id(1)
    nk = BAND_SUB + 2 * BAND_HALF
    row = lax.broadcasted_iota(jnp.int32, (BAND_SUB, nk), 0)
    col = lax.broadcasted_iota(jnp.int32, (BAND_SUB, nk), 1)
    band = jnp.abs(col - BAND_HALF - row) <= BAND_HALF
    lane = lax.broadcasted_iota(jnp.int32, (1, LANES), 1)
    first_head = (lane % 64) < 32
    low_half = lane < 64

    def residue(r, carry):
        q = q_ref[0, 0, r]
        kk = jnp.concatenate([kp_ref[0, 0, r], kc_ref[0, 0, r], kn_ref[0, 0, r]], axis=0)
        vv = jnp.concatenate([vp_ref[0, 0, r], vc_ref[0, 0, r], vn_ref[0, 0, r]], axis=0)
        for sb in range(tq // BAND_SUB):
            a = sb * BAND_SUB
            kidx = i * tq + (a - BAND_HALF) + col
            valid = band & (kidx >= 0) & (kidx < seq)
            for p in range(DIL_SLABS):
                sl = slice(p * LANES, (p + 1) * LANES)
                qp = q[a:a + BAND_SUB, sl]
                kp = kk[a:a + nk, sl]
                vp = vv[a:a + nk, sl]
                res = []
                for sel in (first_head, jnp.logical_not(first_head)):
                    qm = jnp.where(sel, qp, jnp.zeros_like(qp))
                    sc = jnp.where(valid, _dot_nt(qm, kp), NEG_INF)
                    m = jnp.max(sc, axis=-1, keepdims=True)
                    e = jnp.exp(sc - m)
                    l = jnp.sum(e, axis=-1, keepdims=True)
                    o = _dot(e.astype(_bf16), vp) / l
                    res.append((o, m + jnp.log(l)))
                dst = pl.ds(r + a * dil, BAND_SUB, stride=dil) if dil > 1 else pl.ds(a, BAND_SUB)
                o_ref[0, p, dst, :] = jnp.where(low_half, res[0][0], res[1][0])
                lse_ref[0, p, dst, :] = jnp.where(low_half, res[0][1], res[1][1])
        return carry

    lax.fori_loop(0, dil, residue, 0, unroll=min(dil, 2))


def _band_call(qkv, dil):
    _, b, _, seq, w = qkv.shape
    s = seq * dil
    tq = max(BAND_SUB, BAND_ROWS // dil)
    assert tq % BAND_SUB == 0 and seq % tq == 0
    hb = tq // BAND_HALF
    last = seq // BAND_HALF - 1

    def cur(j):
        return pl.BlockSpec((1, 1, dil, tq, w), lambda bi, i: (j, bi, 0, i, 0))

    def prev(j):
        return pl.BlockSpec((1, 1, dil, BAND_HALF, w),
                            lambda bi, i: (j, bi, 0, jnp.maximum(i * hb - 1, 0), 0))

    def nxt(j):
        return pl.BlockSpec((1, 1, dil, BAND_HALF, w),
                            lambda bi, i: (j, bi, 0, jnp.minimum((i + 1) * hb, last), 0))

    out_spec = pl.BlockSpec((1, DIL_SLABS, dil * tq, LANES), lambda bi, i: (bi, 0, i, 0))
    out_sds = jax.ShapeDtypeStruct((b, DIL_SLABS, s, LANES), _f32)
    kern = functools.partial(_band_kernel, dil=dil, tq=tq, seq=seq)
    return pl.pallas_call(
        kern,
        grid=(b, seq // tq),
        in_specs=[cur(0), prev(1), cur(1), nxt(1), prev(2), cur(2), nxt(2)],
        out_specs=[out_spec, out_spec],
        out_shape=[out_sds, out_sds],
        compiler_params=pltpu.CompilerParams(vmem_limit_bytes=VMEM_LIMIT),
        name=f"band_attn_d{dil}",
    )(qkv, qkv, qkv, qkv, qkv, qkv, qkv)


def _out_kernel(x_ref, ot_ref, zm_ref, o0_ref, o1_ref, o2_ref, l0_ref, l1_ref, l2_ref,
                zd_ref, gt_ref, gate_ref, wpa_ref, wpb_ref, wo_ref, gf_ref, y_ref,
                *, final_norm):
    d = x_ref.shape[-1]
    o_mla = jnp.transpose(ot_ref[0].astype(_f32)) * zm_ref[0].astype(_f32)
    a = _dot(o_mla.astype(_bf16), wpa_ref[...])

    slabs = []
    for p in range(DIL_SLABS):
        l0, l1, l2 = l0_ref[0, p], l1_ref[0, p], l2_ref[0, p]
        mx = jnp.maximum(jnp.maximum(l0, l1), l2)
        w0, w1, w2 = jnp.exp(l0 - mx), jnp.exp(l1 - mx), jnp.exp(l2 - mx)
        slabs.append((w0 * o0_ref[0, p] + w1 * o1_ref[0, p] + w2 * o2_ref[0, p]) / (w0 + w1 + w2))
    o_dil = jnp.concatenate(slabs, axis=1) * zd_ref[0].astype(_f32)
    bb = _dot(o_dil.astype(_bf16), wpb_ref[...])

    gt = gt_ref[0].astype(_f32)
    u = gt[:, :d] * a + gt[:, d:] * bb
    y = x_ref[0] + gate_ref[0] * _dot(u.astype(_bf16), wo_ref[...])
    if final_norm:
        y = _rms(y, gf_ref[...])
    y_ref[0] = y


def _out_call(x, ot, zm, outs, lses, zd, gt, gate, lw, g_final, final_norm):
    b, s, d = x.shape
    tm = min(TOKEN_TILE, s)
    tok = lambda w: pl.BlockSpec((1, tm, w), lambda bi, ti: (bi, ti, 0))
    slab = pl.BlockSpec((1, DIL_SLABS, tm, LANES), lambda bi, ti: (bi, 0, ti, 0))
    const = lambda *shape: pl.BlockSpec(shape, lambda bi, ti: (0,) * len(shape))
    kern = functools.partial(_out_kernel, final_norm=final_norm)
    return pl.pallas_call(
        kern,
        grid=(b, s // tm),
        in_specs=[tok(d),
                  pl.BlockSpec((1, MLA_WIDTH, tm), lambda bi, ti: (bi, 0, ti)),
                  tok(MLA_WIDTH),
                  slab, slab, slab, slab, slab, slab,
                  tok(DIL_WIDTH), tok(2 * d),
                  pl.BlockSpec((1, 1, d), lambda bi, ti: (bi, 0, 0)),
                  const(MLA_WIDTH, d), const(DIL_WIDTH, d), const(d, d), const(1, d)],
        out_specs=tok(d),
        out_shape=jax.ShapeDtypeStruct((b, s, d), _f32),
        compiler_params=pltpu.CompilerParams(vmem_limit_bytes=VMEM_LIMIT),
        name="out_proj",
    )(x, ot.reshape(b, MLA_WIDTH, s), zm, outs[0], outs[1], outs[2], lses[0], lses[1], lses[2],
      zd, gt, gate, lw["w_pa"], lw["w_pb"], lw["w_out"], g_final)


def _mla_lane_order():
    half = MLA_ROPE // 2
    src = np.full((HEAD_PAD,), -1, np.int64)
    src[0:half] = MLA_NOPE + np.arange(half)
    src[half:64] = np.arange(64 - half)
    src[64:64 + half] = MLA_NOPE + half + np.arange(half)
    src[64 + half:64 + half + (MLA_NOPE - (64 - half))] = np.arange(64 - half, MLA_NOPE)
    return src


def _gather_cols(w, src):
    picked = jnp.take(w, jnp.asarray(np.maximum(src, 0)), axis=1)
    return jnp.where(jnp.asarray(src >= 0)[None, :], picked, 0.0)


def _dil_perm():
    idx = []
    half = DIL_HEAD_DIM // 2
    for p in range(DIL_HEADS // 2):
        ha, hb = 2 * p * DIL_HEAD_DIM, (2 * p + 1) * DIL_HEAD_DIM
        idx += list(range(ha, ha + half)) + list(range(hb, hb + half))
        idx += list(range(ha + half, ha + 2 * half)) + list(range(hb + half, hb + 2 * half))
    return np.asarray(idx, np.int64)


def _pack_layer(w_in, g_norm, b_gate, g_cq, w_uq, g_ckv, w_ukv, w_pa, w_pb, w_out):
    d = w_in.shape[0]
    splits = np.cumsum([Q_LORA, KV_LORA, MLA_ROPE, MLA_WIDTH] + [DIL_WIDTH] * (3 * N_GROUPS)
                       + [DIL_WIDTH])
    parts = jnp.split(w_in, list(splits), axis=1)
    perm = _dil_perm()
    lane_src = _mla_lane_order()
    kr_src = np.where(lane_src >= MLA_NOPE, lane_src - MLA_NOPE, -1)
    cols = [parts[0], parts[1], _gather_cols(parts[2], kr_src), parts[3]]
    for j in range(3 * N_GROUPS):
        pj = parts[4 + j]
        cols.append(pj if j % 3 == 2 else jnp.take(pj, jnp.asarray(perm), axis=1))
    cols += [parts[4 + 3 * N_GROUPS], parts[5 + 3 * N_GROUPS]]
    w_big = jnp.concatenate(cols, axis=1).astype(_bf16)

    dq = MLA_NOPE + MLA_ROPE
    q_src = np.concatenate([np.where(lane_src >= 0, h * dq + lane_src, -1) for h in range(MLA_HEADS)])
    w_uq_p = _gather_cols(w_uq, q_src).astype(_bf16)
    dkv = MLA_NOPE + MLA_V
    nope_src = np.where((lane_src >= 0) & (lane_src < MLA_NOPE), lane_src, -1)
    k_src = np.concatenate([np.where(nope_src >= 0, h * dkv + nope_src, -1) for h in range(MLA_HEADS)])
    w_k = _gather_cols(w_ukv, k_src).astype(_bf16)
    v_src = np.concatenate([h * dkv + MLA_NOPE + np.arange(MLA_V) for h in range(MLA_HEADS)])
    w_vt = jnp.transpose(jnp.take(w_ukv, jnp.asarray(v_src), axis=1)).astype(_bf16)

    return dict(
        w_big=w_big, g_norm=g_norm.reshape(1, d), b_gate=b_gate.reshape(1, -1),
        g_cq=g_cq.reshape(1, -1), w_uq=w_uq_p, g_ckv=g_ckv.reshape(1, -1), w_k=w_k,
        w_vt=w_vt,
        w_pa=w_pa.astype(_bf16), w_pb=w_pb.astype(_bf16), w_out=w_out.astype(_bf16))


def _rope_tables(s, tm):
    pos = np.arange(s, dtype=np.float64)[:, None]
    lane = np.arange(LANES)

    def table(freq_idx, active, n_freq):
        inv = np.power(ROPE_THETA, -2.0 * freq_idx / (2 * n_freq))
        ang = pos * inv[None, :]
        cos = np.where(active[None, :], np.cos(ang), 1.0)
        sign = np.where(lane < 64, -1.0, 1.0)
        sin = np.where(active[None, :], np.sin(ang) * sign[None, :], 0.0)
        return np.stack([cos, sin]).astype(np.float32)

    half = MLA_ROPE // 2
    tab_m = table((lane % 64).clip(0, half - 1).astype(np.float64), (lane % 64) < half, half)
    tab_d = jnp.asarray(table((lane % 32).astype(np.float64), np.ones(LANES, bool), DIL_HEAD_DIM // 2))
    tabs = []
    for _, dil in DIL_GROUPS:
        t = tab_d.reshape(2, s // tm, tm // dil, dil, LANES)
        tabs.append(jnp.swapaxes(t, 2, 3).reshape(2, s, LANES))
    return jnp.asarray(tab_m), tabs


def _trunk(x, mods, layers, g_final):
    b, s, d = x.shape
    assert s % (DIL_GROUPS[-1][1] * BAND_SUB) == 0 and s % TOKEN_TILE == 0
    tab_m, tabs_d = _rope_tables(s, min(TOKEN_TILE, s))
    depth = len(layers)
    for l, lw in enumerate(layers):
        mod = mods[l]
        shift, scale, gate = [mod[:, None, j * d:(j + 1) * d] for j in range(3)]
        q, k, vt, zm, d1, d4, d16, zd, gt = _in_call(x, scale, shift, lw, tab_m, tabs_d)
        ot = _mla_call(q, k, vt)
        outs, lses = [], []
        for qkv, (_, dl) in zip((d1, d4, d16), DIL_GROUPS):
            o_g, lse_g = _band_call(qkv, dl)
            outs.append(o_g)
            lses.append(lse_g)
        x = _out_call(x, ot, zm, outs, lses, zd, gt, gate, lw, g_final.reshape(1, d),
                      final_norm=(l == depth - 1))
    return x


def kernel(x_prompt, x_sample, c_prompt, c_sample, w_ada, b_ada, g_norm, w_in, b_gate, g_cq, w_uq,
           g_ckv, w_ukv, w_pa, w_pb, w_out, g_final):
    depth = w_in.shape[0]
    bp, bs = c_prompt.shape[0], c_sample.shape[0]
    rows = -(-(bp + bs) // 8) * 8
    c_all = jnp.concatenate([c_prompt, c_sample], axis=0)
    c_all = jnp.pad(c_all, ((0, rows - bp - bs), (0, 0)))
    mods = _ada_call(c_all, w_ada, b_ada)
    layers = [_pack_layer(w_in[l], g_norm[l], b_gate[l], g_cq[l], w_uq[l], g_ckv[l], w_ukv[l],
                          w_pa[l], w_pb[l], w_out[l]) for l in range(depth)]
    y_prompt = _trunk(x_prompt, mods[:, :bp], layers, g_final)
    y_sample = _trunk(x_sample, mods[:, bp:bp + bs], layers, g_final)
    return (y_prompt, y_sample)
```

```python
import functools
import math

import numpy as np
import jax
import jax.numpy as jnp
from jax import lax
from jax.experimental import pallas as pl
from jax.experimental.pallas import tpu as pltpu

ROPE_THETA = 10000.0
EPS = 1e-6
NEG_INF = -1e30

MLA_HEADS = 8
MLA_NOPE = 64
MLA_ROPE = 32
MLA_V = 64
Q_LORA = 384
KV_LORA = 256
MLA_WIDTH = MLA_HEADS * MLA_V
DIL_GROUPS = ((128, 1), (512, 4), (2048, 16))
DIL_HEADS = 8
DIL_HEAD_DIM = 64
DIL_WIDTH = DIL_HEADS * DIL_HEAD_DIM
N_GROUPS = len(DIL_GROUPS)
BAND_HALF = 64

LANES = 128
KR_PAD = LANES
HEAD_PAD = LANES
DIL_SLABS = DIL_WIDTH // LANES

OFF_CQ = 0
OFF_CKV = OFF_CQ + Q_LORA
OFF_KR = OFF_CKV + KV_LORA
OFF_ZM = OFF_KR + KR_PAD
OFF_DIL = OFF_ZM + MLA_WIDTH
OFF_ZD = OFF_DIL + 3 * N_GROUPS * DIL_WIDTH
OFF_MG = OFF_ZD + DIL_WIDTH
W_BIG = OFF_MG + 2 * 1024

TOKEN_TILE = 512
MLA_Q_TILE = 1024
MLA_K_CHUNK = 512
MLA_UNROLL = 4
BAND_SUB = 128
BAND_ROWS = 512
VMEM_LIMIT = 56 * 1024 * 1024

_f32 = jnp.float32
_bf16 = jnp.bfloat16
_NT = (((1,), (1,)), ((), ()))


def _dot(a, b):
    return jnp.dot(a, b, preferred_element_type=_f32)


def _dot_nt(a, b):
    return lax.dot_general(a, b, _NT, preferred_element_type=_f32)


def _rms(x, g):
    return x * lax.rsqrt(jnp.mean(x * x, axis=-1, keepdims=True) + EPS) * g


def _sigmoid(x):
    return 1.0 / (1.0 + jnp.exp(-x))


def _rope_pairs(x, cos, sin):
    return x * cos + pltpu.roll(x, 64, axis=1) * sin


def _ada_kernel(c_ref, w_ref, b_ref, o_ref):
    c = c_ref[...]
    a = c * _sigmoid(c)
    a_hi = a.astype(_bf16)
    a_lo = (a - a_hi.astype(_f32)).astype(_bf16)
    w = w_ref[0]
    w_hi = w.astype(_bf16)
    w_lo = (w - w_hi.astype(_f32)).astype(_bf16)
    o_ref[0] = _dot(a_hi, w_hi) + _dot(a_hi, w_lo) + _dot(a_lo, w_hi) + b_ref[0]


def _ada_call(c_all, w_ada, b_ada):
    depth, d, n = w_ada.shape
    rows = c_all.shape[0]
    nb = n // d
    return pl.pallas_call(
        _ada_kernel,
        grid=(depth, nb),
        in_specs=[
            pl.BlockSpec((rows, d), lambda l, j: (0, 0)),
            pl.BlockSpec((1, d, d), lambda l, j: (l, 0, j)),
            pl.BlockSpec((1, 1, d), lambda l, j: (l, 0, j)),
        ],
        out_specs=pl.BlockSpec((1, rows, d), lambda l, j: (l, 0, j)),
        out_shape=jax.ShapeDtypeStruct((depth, rows, n), _f32),
        compiler_params=pltpu.CompilerParams(vmem_limit_bytes=VMEM_LIMIT),
        name="adaln",
    )(c_all, w_ada, b_ada.reshape(depth, 1, n))


def _in_kernel(x_ref, sc_ref, sh_ref, gn_ref, w_ref, gcq_ref, wuq_ref, gckv_ref, wk_ref,
               wvt_ref, bg_ref, tm_ref, t1_ref, t4_ref, t16_ref,
               q_ref, k_ref, vt_ref, zm_ref, d1_ref, d4_ref, d16_ref, zd_ref, gt_ref,
               xs_ref, *, q_scale):
    x = x_ref[0]
    rows, d_model = x.shape

    def modulate(xv):
        return (_rms(xv, gn_ref[...]) * (1.0 + sc_ref[0]) + sh_ref[0]).astype(_bf16)

    hb = modulate(x)
    cos_m, sin_m = tm_ref[0], tm_ref[1]

    cq = _dot(hb, w_ref[:, OFF_CQ:OFF_CQ + Q_LORA])
    cqn = _rms(cq, gcq_ref[...]).astype(_bf16)
    q = _dot(cqn, wuq_ref[...])
    for hd in range(MLA_HEADS):
        sl = slice(hd * HEAD_PAD, (hd + 1) * HEAD_PAD)
        q_ref[0, hd] = (_rope_pairs(q[:, sl], cos_m, sin_m) * q_scale).astype(_bf16)

    ckv = _dot(hb, w_ref[:, OFF_CKV:OFF_CKV + KV_LORA])
    ckvn = _rms(ckv, gckv_ref[...]).astype(_bf16)
    kr = _dot(hb, w_ref[:, OFF_KR:OFF_KR + KR_PAD])
    k = _dot(ckvn, wk_ref[...])
    for hd in range(MLA_HEADS):
        sl = slice(hd * HEAD_PAD, (hd + 1) * HEAD_PAD)
        k_ref[0, hd] = _rope_pairs(k[:, sl] + kr, cos_m, sin_m).astype(_bf16)
    vt = _dot_nt(wvt_ref[...], ckvn)
    for hd in range(MLA_HEADS):
        vt_ref[0, hd, 0] = vt[hd * MLA_V:(hd + 1) * MLA_V, :].astype(_bf16)

    zm = _dot(hb, w_ref[:, OFF_ZM:OFF_ZM + MLA_WIDTH])
    zm_ref[0] = (zm * _sigmoid(zm)).astype(_bf16)

    zd = _dot(hb, w_ref[:, OFF_ZD:OFF_ZD + DIL_WIDTH])
    zd_ref[0] = (zd * _sigmoid(zd)).astype(_bf16)

    mg = _dot(hb, w_ref[:, OFF_MG:W_BIG]) + bg_ref[...]
    gt_ref[0] = _sigmoid(mg).astype(_bf16)

    for c in range(d_model // LANES):
        xs_ref[c] = x[:, c * LANES:(c + 1) * LANES]
    for g, (dil, out_ref, tab_ref) in enumerate(((1, d1_ref, t1_ref), (4, d4_ref, t4_ref),
                                                 (16, d16_ref, t16_ref))):
        per = rows // dil
        if dil == 1:
            hg = hb
        else:
            xg = jnp.concatenate(
                [jnp.concatenate([xs_ref[c, pl.ds(r, per, stride=dil), :] for r in range(dil)], axis=0)
                 for c in range(d_model // LANES)], axis=1)
            hg = modulate(xg)
        cos_d, sin_d = tab_ref[0], tab_ref[1]
        for kind in range(3):
            off = OFF_DIL + (3 * g + kind) * DIL_WIDTH
            u = _dot(hg, w_ref[:, off:off + DIL_WIDTH])
            if kind == 2:
                ub = u.astype(_bf16)
            else:
                post = DIL_HEAD_DIM ** -0.5 * math.log2(math.e) if kind == 0 else 1.0
                ub = jnp.concatenate(
                    [(_rope_pairs(u[:, p * LANES:(p + 1) * LANES], cos_d, sin_d) * post).astype(_bf16)
                     for p in range(DIL_SLABS)], axis=1)
            for r in range(dil):
                out_ref[kind, 0, r] = ub[r * per:(r + 1) * per]


def _in_call(x, scale, shift, lw, tab_m, tabs_d):
    b, s, d = x.shape
    tm = min(TOKEN_TILE, s)
    nt = s // tm
    const = lambda *shape: pl.BlockSpec(shape, lambda bi, ti: (0,) * len(shape),
                                        pipeline_mode=pl.Buffered(1))
    tab = pl.BlockSpec((2, tm, LANES), lambda bi, ti: (0, ti, 0))
    in_specs = [
        pl.BlockSpec((1, tm, d), lambda bi, ti: (bi, ti, 0)),
        pl.BlockSpec((1, 1, d), lambda bi, ti: (bi, 0, 0)),
        pl.BlockSpec((1, 1, d), lambda bi, ti: (bi, 0, 0)),
        const(1, d),
        const(d, W_BIG),
        const(1, Q_LORA),
        const(Q_LORA, MLA_HEADS * HEAD_PAD),
        const(1, KV_LORA),
        const(KV_LORA, MLA_HEADS * HEAD_PAD),
        const(MLA_WIDTH, KV_LORA),
        const(1, 2 * d),
        tab, tab, tab, tab,
    ]
    dil_shapes = [jax.ShapeDtypeStruct((3, b, dl, s // dl, DIL_WIDTH), _bf16) for _, dl in DIL_GROUPS]
    dil_specs = [pl.BlockSpec((3, 1, dl, tm // dl, DIL_WIDTH), lambda bi, ti: (0, bi, 0, ti, 0))
                 for _, dl in DIL_GROUPS]
    out_shape = [
        jax.ShapeDtypeStruct((b, MLA_HEADS, s, HEAD_PAD), _bf16),
        jax.ShapeDtypeStruct((b, MLA_HEADS, s, HEAD_PAD), _bf16),
        jax.ShapeDtypeStruct((b, MLA_HEADS, nt, MLA_V, tm), _bf16),
        jax.ShapeDtypeStruct((b, s, MLA_WIDTH), _bf16),
        *dil_shapes,
        jax.ShapeDtypeStruct((b, s, DIL_WIDTH), _bf16),
        jax.ShapeDtypeStruct((b, s, 2 * d), _bf16),
    ]
    out_specs = [
        pl.BlockSpec((1, MLA_HEADS, tm, HEAD_PAD), lambda bi, ti: (bi, 0, ti, 0)),
        pl.BlockSpec((1, MLA_HEADS, tm, HEAD_PAD), lambda bi, ti: (bi, 0, ti, 0)),
        pl.BlockSpec((1, MLA_HEADS, 1, MLA_V, tm), lambda bi, ti: (bi, 0, ti, 0, 0)),
        pl.BlockSpec((1, tm, MLA_WIDTH), lambda bi, ti: (bi, ti, 0)),
        *dil_specs,
        pl.BlockSpec((1, tm, DIL_WIDTH), lambda bi, ti: (bi, ti, 0)),
        pl.BlockSpec((1, tm, 2 * d), lambda bi, ti: (bi, ti, 0)),
    ]
    kern = functools.partial(_in_kernel, q_scale=(MLA_NOPE + MLA_ROPE) ** -0.5 * math.log2(math.e))
    return pl.pallas_call(
        kern,
        grid=(b, nt),
        in_specs=in_specs,
        out_specs=out_specs,
        out_shape=out_shape,
        scratch_shapes=[pltpu.VMEM((d // LANES, tm, LANES), _f32)],
        compiler_params=pltpu.CompilerParams(vmem_limit_bytes=VMEM_LIMIT),
        name="in_proj",
    )(x, scale, shift, lw["g_norm"], lw["w_big"], lw["g_cq"], lw["w_uq"], lw["g_ckv"],
      lw["w_k"], lw["w_vt"], lw["b_gate"], tab_m, *tabs_d)


def _mla_kernel(q_ref, k_ref, vt_ref, o_ref, sa_ref, sb_ref, m_ref, acc_ref, *, n_tiles, tq,
                n_chunks, tk, unroll):
    ones = jnp.ones((16, tk), _bf16)
    per = tk // vt_ref.shape[-1]
    bufs = (sa_ref, sb_ref)

    def scores(qi, c, dst_ref):
        q = q_ref[0, 0, pl.ds(pl.multiple_of(qi * tq, tq), tq), :]
        kc = k_ref[0, 0, pl.ds(pl.multiple_of(c * tk, tk), tk), :]
        st = _dot_nt(kc, q)
        dst_ref[...] = st
        return jnp.max(st, axis=0, keepdims=True)

    def consume(c, src_ref, cmax):
        m_old = m_ref[...]
        m_new = jnp.maximum(m_old, cmax)
        alpha = jnp.exp2(m_old - m_new)
        p = jnp.exp2(src_ref[...] - m_new).astype(_bf16)
        vc = jnp.concatenate([vt_ref[0, 0, c * per + j] for j in range(per)], axis=1)
        va = jnp.concatenate([vc, ones], axis=0)
        acc_ref[...] = alpha * acc_ref[...] + _dot(va, p)
        m_ref[...] = m_new

    def steps(qi, first, count, cm):
        for u in range(count):
            cm_next = scores(qi, first + u + 1, bufs[(u + 1) % 2])
            consume(first + u, bufs[u % 2], cm)
            cm = cm_next
        return cm

    def tile(qi, cm):
        m_ref[...] = jnp.full(m_ref.shape, NEG_INF, _f32)
        acc_ref[...] = jnp.zeros(acc_ref.shape, _f32)
        cm = lax.fori_loop(0, n_chunks // unroll - 1,
                           lambda j, cm: steps(qi, j * unroll, unroll, cm), cm)
        cm = steps(qi, n_chunks - unroll, unroll - 1, cm)
        cm_next = scores(jnp.minimum(qi + 1, n_tiles - 1), 0, bufs[0])
        consume(n_chunks - 1, bufs[1], cm)
        acc = acc_ref[...]
        o_ref[0, 0, qi] = (acc[:MLA_V] / acc[MLA_V:MLA_V + 1]).astype(_bf16)
        return cm_next

    lax.fori_loop(0, n_tiles, tile, scores(0, 0, sa_ref))


def _mla_call(q, k, vt):
    b, hh, s, _ = q.shape
    nv, tv = vt.shape[2], vt.shape[4]
    tk = min(MLA_K_CHUNK, s // 2)
    n_chunks = s // tk
    unroll = MLA_UNROLL if n_chunks >= 4 * MLA_UNROLL else 2
    assert n_chunks % unroll == 0 and unroll % 2 == 0 and tk % tv == 0
    tq = min(MLA_Q_TILE, s)
    n_tiles = s // tq
    kern = functools.partial(_mla_kernel, n_tiles=n_tiles, tq=tq, n_chunks=n_chunks, tk=tk,
                             unroll=unroll)
    return pl.pallas_call(
        kern,
        grid=(b, hh),
        in_specs=[
            pl.BlockSpec((1, 1, s, HEAD_PAD), lambda bi, hi: (bi, hi, 0, 0)),
            pl.BlockSpec((1, 1, s, HEAD_PAD), lambda bi, hi: (bi, hi, 0, 0)),
            pl.BlockSpec((1, 1, nv, MLA_V, tv), lambda bi, hi: (bi, hi, 0, 0, 0)),
        ],
        out_specs=pl.BlockSpec((1, 1, n_tiles, MLA_V, tq), lambda bi, hi: (bi, hi, 0, 0, 0)),
        out_shape=jax.ShapeDtypeStruct((b, hh, n_tiles, MLA_V, tq), _bf16),
        scratch_shapes=[pltpu.VMEM((tk, tq), _f32), pltpu.VMEM((tk, tq), _f32),
                        pltpu.VMEM((1, tq), _f32), pltpu.VMEM((MLA_V + 16, tq), _f32)],
        compiler_params=pltpu.CompilerParams(vmem_limit_bytes=VMEM_LIMIT),
        name="mla_attn",
    )(q, k, vt)


def _band_kernel(q_ref, kp_ref, kc_ref, kn_ref, vp_ref, vc_ref, vn_ref, o_ref, lse_ref,
                 *, dil, tq, seq):
    i = pl.program_id(1)
    nk = BAND_SUB + 2 * BAND_HALF
    row = lax.broadcasted_iota(jnp.int32, (BAND_SUB, nk), 0)
    col = lax.broadcasted_iota(jnp.int32, (BAND_SUB, nk), 1)
    band = jnp.abs(col - BAND_HALF - row) <= BAND_HALF
    lane = lax.broadcasted_iota(jnp.int32, (1, LANES), 1)
    first_head = (lane % 64) < 32
    low_half = lane < 64

    def residue(r, carry):
        q = q_ref[0, 0, r]
        kk = jnp.concatenate([kp_ref[0, 0, r], kc_ref[0, 0, r], kn_ref[0, 0, r]], axis=0)
        vv = jnp.concatenate([vp_ref[0, 0, r], vc_ref[0, 0, r], vn_ref[0, 0, r]], axis=0)
        for sb in range(tq // BAND_SUB):
            a = sb * BAND_SUB
            kidx = i * tq + (a - BAND_HALF) + col
            valid = band & (kidx >= 0) & (kidx < seq)
            for p in range(DIL_SLABS):
                sl = slice(p * LANES, (p + 1) * LANES)
                qp = q[a:a + BAND_SUB, sl]
                kp = kk[a:a + nk, sl]
                vp = vv[a:a + nk, sl]
                res = []
                for sel in (first_head, jnp.logical_not(first_head)):
                    qm = jnp.where(sel, qp, jnp.zeros_like(qp))
                    sc = jnp.where(valid, _dot_nt(qm, kp), NEG_INF)
                    m = jnp.max(sc, axis=-1, keepdims=True)
                    e = jnp.exp2(sc - m)
                    l = jnp.sum(e, axis=-1, keepdims=True)
                    o = _dot(e.astype(_bf16), vp) / l
                    res.append((o, m + jnp.log2(l)))
                dst = pl.ds(r + a * dil, BAND_SUB, stride=dil) if dil > 1 else pl.ds(a, BAND_SUB)
                o_ref[0, p, dst, :] = jnp.where(low_half, res[0][0], res[1][0])
                lse_ref[0, p, dst, :] = jnp.where(low_half, res[0][1], res[1][1])
        return carry

    lax.fori_loop(0, dil, residue, 0, unroll=min(dil, 2))


def _band_call(qkv, dil):
    _, b, _, seq, w = qkv.shape
    s = seq * dil
    tq = max(BAND_SUB, BAND_ROWS // dil)
    assert tq % BAND_SUB == 0 and seq % tq == 0
    hb = tq // BAND_HALF
    last = seq // BAND_HALF - 1

    def cur(j):
        return pl.BlockSpec((1, 1, dil, tq, w), lambda bi, i: (j, bi, 0, i, 0))

    def prev(j):
        return pl.BlockSpec((1, 1, dil, BAND_HALF, w),
                            lambda bi, i: (j, bi, 0, jnp.maximum(i * hb - 1, 0), 0))

    def nxt(j):
        return pl.BlockSpec((1, 1, dil, BAND_HALF, w),
                            lambda bi, i: (j, bi, 0, jnp.minimum((i + 1) * hb, last), 0))

    out_spec = pl.BlockSpec((1, DIL_SLABS, dil * tq, LANES), lambda bi, i: (bi, 0, i, 0))
    out_sds = jax.ShapeDtypeStruct((b, DIL_SLABS, s, LANES), _f32)
    kern = functools.partial(_band_kernel, dil=dil, tq=tq, seq=seq)
    return pl.pallas_call(
        kern,
        grid=(b, seq // tq),
        in_specs=[cur(0), prev(1), cur(1), nxt(1), prev(2), cur(2), nxt(2)],
        out_specs=[out_spec, out_spec],
        out_shape=[out_sds, out_sds],
        compiler_params=pltpu.CompilerParams(vmem_limit_bytes=VMEM_LIMIT),
        name=f"band_attn_d{dil}",
    )(qkv, qkv, qkv, qkv, qkv, qkv, qkv)


def _out_kernel(x_ref, ot_ref, zm_ref, o0_ref, o1_ref, o2_ref, l0_ref, l1_ref, l2_ref,
                zd_ref, gt_ref, gate_ref, wpa_ref, wpb_ref, wo_ref, gf_ref, y_ref,
                *, final_norm):
    d = x_ref.shape[-1]
    ot = jnp.concatenate([ot_ref[0, hd, 0] for hd in range(MLA_HEADS)], axis=0)
    o_mla = jnp.transpose(ot.astype(_f32)) * zm_ref[0].astype(_f32)
    a = _dot(o_mla.astype(_bf16), wpa_ref[...])

    slabs = []
    for p in range(DIL_SLABS):
        l0, l1, l2 = l0_ref[0, p], l1_ref[0, p], l2_ref[0, p]
        mx = jnp.maximum(jnp.maximum(l0, l1), l2)
        w0, w1, w2 = jnp.exp2(l0 - mx), jnp.exp2(l1 - mx), jnp.exp2(l2 - mx)
        slabs.append((w0 * o0_ref[0, p] + w1 * o1_ref[0, p] + w2 * o2_ref[0, p]) / (w0 + w1 + w2))
    o_dil = jnp.concatenate(slabs, axis=1) * zd_ref[0].astype(_f32)
    bb = _dot(o_dil.astype(_bf16), wpb_ref[...])

    gt = gt_ref[0].astype(_f32)
    u = gt[:, :d] * a + gt[:, d:] * bb
    y = x_ref[0] + gate_ref[0] * _dot(u.astype(_bf16), wo_ref[...])
    if final_norm:
        y = _rms(y, gf_ref[...])
    y_ref[0] = y


def _out_call(x, ot, zm, outs, lses, zd, gt, gate, lw, g_final, final_norm):
    b, s, d = x.shape
    tm = min(TOKEN_TILE, s)
    tok = lambda w: pl.BlockSpec((1, tm, w), lambda bi, ti: (bi, ti, 0))
    slab = pl.BlockSpec((1, DIL_SLABS, tm, LANES), lambda bi, ti: (bi, 0, ti, 0))
    const = lambda *shape: pl.BlockSpec(shape, lambda bi, ti: (0,) * len(shape))
    kern = functools.partial(_out_kernel, final_norm=final_norm)
    per = ot.shape[-1] // tm
    return pl.pallas_call(
        kern,
        grid=(b, s // tm),
        in_specs=[tok(d),
                  pl.BlockSpec((1, MLA_HEADS, 1, MLA_V, tm),
                               lambda bi, ti: (bi, 0, ti // per, 0, ti % per)),
                  tok(MLA_WIDTH),
                  slab, slab, slab, slab, slab, slab,
                  tok(DIL_WIDTH), tok(2 * d),
                  pl.BlockSpec((1, 1, d), lambda bi, ti: (bi, 0, 0)),
                  const(MLA_WIDTH, d), const(DIL_WIDTH, d), const(d, d), const(1, d)],
        out_specs=tok(d),
        out_shape=jax.ShapeDtypeStruct((b, s, d), _f32),
        compiler_params=pltpu.CompilerParams(vmem_limit_bytes=VMEM_LIMIT),
        name="out_proj",
    )(x, ot, zm, outs[0], outs[1], outs[2], lses[0], lses[1], lses[2],
      zd, gt, gate, lw["w_pa"], lw["w_pb"], lw["w_out"], g_final)


def _mla_lane_order():
    half = MLA_ROPE // 2
    src = np.full((HEAD_PAD,), -1, np.int64)
    src[0:half] = MLA_NOPE + np.arange(half)
    src[half:64] = np.arange(64 - half)
    src[64:64 + half] = MLA_NOPE + half + np.arange(half)
    src[64 + half:64 + half + (MLA_NOPE - (64 - half))] = np.arange(64 - half, MLA_NOPE)
    return src


def _gather_cols(w, src):
    picked = jnp.take(w, jnp.asarray(np.maximum(src, 0)), axis=1)
    return jnp.where(jnp.asarray(src >= 0)[None, :], picked, 0.0)


def _dil_perm():
    idx = []
    half = DIL_HEAD_DIM // 2
    for p in range(DIL_HEADS // 2):
        ha, hb = 2 * p * DIL_HEAD_DIM, (2 * p + 1) * DIL_HEAD_DIM
        idx += list(range(ha, ha + half)) + list(range(hb, hb + half))
        idx += list(range(ha + half, ha + 2 * half)) + list(range(hb + half, hb + 2 * half))
    return np.asarray(idx, np.int64)


def _pack_layer(w_in, g_norm, b_gate, g_cq, w_uq, g_ckv, w_ukv, w_pa, w_pb, w_out):
    d = w_in.shape[0]
    splits = np.cumsum([Q_LORA, KV_LORA, MLA_ROPE, MLA_WIDTH] + [DIL_WIDTH] * (3 * N_GROUPS)
                       + [DIL_WIDTH])
    parts = jnp.split(w_in, list(splits), axis=1)
    perm = _dil_perm()
    lane_src = _mla_lane_order()
    kr_src = np.where(lane_src >= MLA_NOPE, lane_src - MLA_NOPE, -1)
    cols = [parts[0], parts[1], _gather_cols(parts[2], kr_src), parts[3]]
    for j in range(3 * N_GROUPS):
        pj = parts[4 + j]
        cols.append(pj if j % 3 == 2 else jnp.take(pj, jnp.asarray(perm), axis=1))
    cols += [parts[4 + 3 * N_GROUPS], parts[5 + 3 * N_GROUPS]]
    w_big = jnp.concatenate(cols, axis=1).astype(_bf16)

    dq = MLA_NOPE + MLA_ROPE
    q_src = np.concatenate([np.where(lane_src >= 0, h * dq + lane_src, -1) for h in range(MLA_HEADS)])
    w_uq_p = _gather_cols(w_uq, q_src).astype(_bf16)
    dkv = MLA_NOPE + MLA_V
    nope_src = np.where((lane_src >= 0) & (lane_src < MLA_NOPE), lane_src, -1)
    k_src = np.concatenate([np.where(nope_src >= 0, h * dkv + nope_src, -1) for h in range(MLA_HEADS)])
    w_k = _gather_cols(w_ukv, k_src).astype(_bf16)
    v_src = np.concatenate([h * dkv + MLA_NOPE + np.arange(MLA_V) for h in range(MLA_HEADS)])
    w_vt = jnp.transpose(jnp.take(w_ukv, jnp.asarray(v_src), axis=1)).astype(_bf16)

    return dict(
        w_big=w_big, g_norm=g_norm.reshape(1, d), b_gate=b_gate.reshape(1, -1),
        g_cq=g_cq.reshape(1, -1), w_uq=w_uq_p, g_ckv=g_ckv.reshape(1, -1), w_k=w_k,
        w_vt=w_vt,
        w_pa=w_pa.astype(_bf16), w_pb=w_pb.astype(_bf16), w_out=w_out.astype(_bf16))


def _rope_tables(s, tm):
    pos = np.arange(s, dtype=np.float64)[:, None]
    lane = np.arange(LANES)

    def table(freq_idx, active, n_freq):
        inv = np.power(ROPE_THETA, -2.0 * freq_idx / (2 * n_freq))
        ang = pos * inv[None, :]
        cos = np.where(active[None, :], np.cos(ang), 1.0)
        sign = np.where(lane < 64, -1.0, 1.0)
        sin = np.where(active[None, :], np.sin(ang) * sign[None, :], 0.0)
        return np.stack([cos, sin]).astype(np.float32)

    half = MLA_ROPE // 2
    tab_m = table((lane % 64).clip(0, half - 1).astype(np.float64), (lane % 64) < half, half)
    tab_d = jnp.asarray(table((lane % 32).astype(np.float64), np.ones(LANES, bool), DIL_HEAD_DIM // 2))
    tabs = []
    for _, dil in DIL_GROUPS:
        t = tab_d.reshape(2, s // tm, tm // dil, dil, LANES)
        tabs.append(jnp.swapaxes(t, 2, 3).reshape(2, s, LANES))
    return jnp.asarray(tab_m), tabs


def _trunk(x, mods, layers, g_final):
    b, s, d = x.shape
    assert s % (DIL_GROUPS[-1][1] * BAND_SUB) == 0 and s % TOKEN_TILE == 0
    tab_m, tabs_d = _rope_tables(s, min(TOKEN_TILE, s))
    depth = len(layers)
    for l, lw in enumerate(layers):
        mod = mods[l]
        shift, scale, gate = [mod[:, None, j * d:(j + 1) * d] for j in range(3)]
        q, k, vt, zm, d1, d4, d16, zd, gt = _in_call(x, scale, shift, lw, tab_m, tabs_d)
        ot = _mla_call(q, k, vt)
        outs, lses = [], []
        for qkv, (_, dl) in zip((d1, d4, d16), DIL_GROUPS):
            o_g, lse_g = _band_call(qkv, dl)
            outs.append(o_g)
            lses.append(lse_g)
        x = _out_call(x, ot, zm, outs, lses, zd, gt, gate, lw, g_final.reshape(1, d),
                      final_norm=(l == depth - 1))
    return x


def kernel(x_prompt, x_sample, c_prompt, c_sample, w_ada, b_ada, g_norm, w_in, b_gate, g_cq, w_uq,
           g_ckv, w_ukv, w_pa, w_pb, w_out, g_final):
    depth = w_in.shape[0]
    bp, bs = c_prompt.shape[0], c_sample.shape[0]
    rows = -(-(bp + bs) // 8) * 8
    c_all = jnp.concatenate([c_prompt, c_sample], axis=0)
    c_all = jnp.pad(c_all, ((0, rows - bp - bs), (0, 0)))
    mods = _ada_call(c_all, w_ada, b_ada)
    layers = [_pack_layer(w_in[l], g_norm[l], b_gate[l], g_cq[l], w_uq[l], g_ckv[l], w_ukv[l],
                          w_pa[l], w_pb[l], w_out[l]) for l in range(depth)]
    y_prompt = _trunk(x_prompt, mods[:, :bp], layers, g_final)
    y_sample = _trunk(x_sample, mods[:, bp:bp + bs], layers, g_final)
    return (y_prompt, y_sample)
```

```python
import functools
import math

import numpy as np
import jax
import jax.numpy as jnp
from jax import lax
from jax.experimental import pallas as pl
from jax.experimental.pallas import tpu as pltpu

ROPE_THETA = 10000.0
EPS = 1e-6
NEG_INF = -1e30

MLA_HEADS = 8
MLA_NOPE = 64
MLA_ROPE = 32
MLA_V = 64
Q_LORA = 384
KV_LORA = 256
MLA_WIDTH = MLA_HEADS * MLA_V
DIL_GROUPS = ((128, 1), (512, 4), (2048, 16))
DIL_HEADS = 8
DIL_HEAD_DIM = 64
DIL_WIDTH = DIL_HEADS * DIL_HEAD_DIM
N_GROUPS = len(DIL_GROUPS)
BAND_HALF = 64

LANES = 128
KR_PAD = LANES
HEAD_PAD = LANES
DIL_SLABS = DIL_WIDTH // LANES
LSE_LANES = LANES // DIL_HEADS

OFF_CQ = 0
OFF_CKV = OFF_CQ + Q_LORA
OFF_KR = OFF_CKV + KV_LORA
OFF_ZM = OFF_KR + KR_PAD
OFF_DIL = OFF_ZM + MLA_WIDTH
OFF_ZD = OFF_DIL + 3 * N_GROUPS * DIL_WIDTH
OFF_MG = OFF_ZD + DIL_WIDTH
W_BIG = OFF_MG + 2 * 1024

TOKEN_TILE = 512
MLA_Q_TILE = 1024
MLA_K_CHUNK = 512
MLA_UNROLL = 4
BAND_SUB = 128
BAND_ROWS = 512
VMEM_LIMIT = 56 * 1024 * 1024

_f32 = jnp.float32
_bf16 = jnp.bfloat16
_NT = (((1,), (1,)), ((), ()))


def _dot(a, b):
    return jnp.dot(a, b, preferred_element_type=_f32)


def _dot_nt(a, b):
    return lax.dot_general(a, b, _NT, preferred_element_type=_f32)


def _rms(x, g):
    return x * lax.rsqrt(jnp.mean(x * x, axis=-1, keepdims=True) + EPS) * g


def _sigmoid(x):
    return 1.0 / (1.0 + jnp.exp(-x))


def _rope_pairs(x, cos, sin):
    return x * cos + pltpu.roll(x, 64, axis=1) * sin


def _ada_kernel(c_ref, w_ref, b_ref, o_ref):
    c = c_ref[...]
    a = c * _sigmoid(c)
    a_hi = a.astype(_bf16)
    a_lo = (a - a_hi.astype(_f32)).astype(_bf16)
    w = w_ref[0]
    w_hi = w.astype(_bf16)
    w_lo = (w - w_hi.astype(_f32)).astype(_bf16)
    o_ref[0] = _dot(a_hi, w_hi) + _dot(a_hi, w_lo) + _dot(a_lo, w_hi) + b_ref[0]


def _ada_call(c_all, w_ada, b_ada):
    depth, d, n = w_ada.shape
    rows = c_all.shape[0]
    nb = n // d
    return pl.pallas_call(
        _ada_kernel,
        grid=(depth, nb),
        in_specs=[
            pl.BlockSpec((rows, d), lambda l, j: (0, 0)),
            pl.BlockSpec((1, d, d), lambda l, j: (l, 0, j)),
            pl.BlockSpec((1, 1, d), lambda l, j: (l, 0, j)),
        ],
        out_specs=pl.BlockSpec((1, rows, d), lambda l, j: (l, 0, j)),
        out_shape=jax.ShapeDtypeStruct((depth, rows, n), _f32),
        compiler_params=pltpu.CompilerParams(vmem_limit_bytes=VMEM_LIMIT),
        name="adaln",
    )(c_all, w_ada, b_ada.reshape(depth, 1, n))


def _in_kernel(x_ref, sc_ref, sh_ref, gn_ref, w_ref, gcq_ref, wuq_ref, gckv_ref, wk_ref,
               wvt_ref, bg_ref, tm_ref, t1_ref, t4_ref, t16_ref,
               q_ref, k_ref, vt_ref, zm_ref, d1_ref, d4_ref, d16_ref, zd_ref, gt_ref,
               hs_ref, *, q_scale):
    x = x_ref[0]
    rows, d_model = x.shape
    h = _rms(x, gn_ref[...]) * (1.0 + sc_ref[0]) + sh_ref[0]
    hb = h.astype(_bf16)
    for c in range(d_model // LANES):
        hs_ref[c] = h[:, c * LANES:(c + 1) * LANES]
    cos_m, sin_m = tm_ref[0], tm_ref[1]

    cq = _dot(hb, w_ref[:, OFF_CQ:OFF_CQ + Q_LORA])
    cqn = _rms(cq, gcq_ref[...]).astype(_bf16)
    q = _dot(cqn, wuq_ref[...])
    for hd in range(MLA_HEADS):
        sl = slice(hd * HEAD_PAD, (hd + 1) * HEAD_PAD)
        q_ref[0, hd] = (_rope_pairs(q[:, sl], cos_m, sin_m) * q_scale).astype(_bf16)

    ckv = _dot(hb, w_ref[:, OFF_CKV:OFF_CKV + KV_LORA])
    ckvn = _rms(ckv, gckv_ref[...]).astype(_bf16)
    kr = _dot(hb, w_ref[:, OFF_KR:OFF_KR + KR_PAD])
    k = _dot(ckvn, wk_ref[...])
    for hd in range(MLA_HEADS):
        sl = slice(hd * HEAD_PAD, (hd + 1) * HEAD_PAD)
        k_ref[0, hd] = _rope_pairs(k[:, sl] + kr, cos_m, sin_m).astype(_bf16)
    vt = _dot_nt(wvt_ref[...], ckvn)
    for hd in range(MLA_HEADS):
        vt_ref[0, hd, 0] = vt[hd * MLA_V:(hd + 1) * MLA_V, :].astype(_bf16)

    zm = _dot(hb, w_ref[:, OFF_ZM:OFF_ZM + MLA_WIDTH])
    zm_ref[0] = (zm * _sigmoid(zm)).astype(_bf16)

    zd = _dot(hb, w_ref[:, OFF_ZD:OFF_ZD + DIL_WIDTH])
    zd_ref[0] = (zd * _sigmoid(zd)).astype(_bf16)

    mg = _dot(hb, w_ref[:, OFF_MG:W_BIG]) + bg_ref[...]
    gt_ref[0] = _sigmoid(mg).astype(_bf16)

    for g, (dil, out_ref, tab_ref) in enumerate(((1, d1_ref, t1_ref), (4, d4_ref, t4_ref),
                                                 (16, d16_ref, t16_ref))):
        per = rows // dil
        if dil == 1:
            hg = hb
        else:
            hg = jnp.concatenate(
                [jnp.concatenate([hs_ref[c, pl.ds(r, per, stride=dil), :] for r in range(dil)], axis=0)
                 for c in range(d_model // LANES)], axis=1).astype(_bf16)
        cos_d, sin_d = tab_ref[0], tab_ref[1]
        for kind in range(3):
            off = OFF_DIL + (3 * g + kind) * DIL_WIDTH
            u = _dot(hg, w_ref[:, off:off + DIL_WIDTH])
            if kind == 2:
                ub = u.astype(_bf16)
            else:
                post = DIL_HEAD_DIM ** -0.5 * math.log2(math.e) if kind == 0 else 1.0
                ub = jnp.concatenate(
                    [(_rope_pairs(u[:, p * LANES:(p + 1) * LANES], cos_d, sin_d) * post).astype(_bf16)
                     for p in range(DIL_SLABS)], axis=1)
            for r in range(dil):
                out_ref[kind, 0, r] = ub[r * per:(r + 1) * per]


def _in_call(x, scale, shift, lw, tab_m, tabs_d):
    b, s, d = x.shape
    tm = min(TOKEN_TILE, s)
    nt = s // tm
    const = lambda *shape: pl.BlockSpec(shape, lambda bi, ti: (0,) * len(shape),
                                        pipeline_mode=pl.Buffered(1))
    tab = pl.BlockSpec((2, tm, LANES), lambda bi, ti: (0, ti, 0))
    in_specs = [
        pl.BlockSpec((1, tm, d), lambda bi, ti: (bi, ti, 0)),
        pl.BlockSpec((1, 1, d), lambda bi, ti: (bi, 0, 0)),
        pl.BlockSpec((1, 1, d), lambda bi, ti: (bi, 0, 0)),
        const(1, d),
        const(d, W_BIG),
        const(1, Q_LORA),
        const(Q_LORA, MLA_HEADS * HEAD_PAD),
        const(1, KV_LORA),
        const(KV_LORA, MLA_HEADS * HEAD_PAD),
        const(MLA_WIDTH, KV_LORA),
        const(1, 2 * d),
        tab, tab, tab, tab,
    ]
    dil_shapes = [jax.ShapeDtypeStruct((3, b, dl, s // dl, DIL_WIDTH), _bf16) for _, dl in DIL_GROUPS]
    dil_specs = [pl.BlockSpec((3, 1, dl, tm // dl, DIL_WIDTH), lambda bi, ti: (0, bi, 0, ti, 0))
                 for _, dl in DIL_GROUPS]
    out_shape = [
        jax.ShapeDtypeStruct((b, MLA_HEADS, s, HEAD_PAD), _bf16),
        jax.ShapeDtypeStruct((b, MLA_HEADS, s, HEAD_PAD), _bf16),
        jax.ShapeDtypeStruct((b, MLA_HEADS, nt, MLA_V, tm), _bf16),
        jax.ShapeDtypeStruct((b, s, MLA_WIDTH), _bf16),
        *dil_shapes,
        jax.ShapeDtypeStruct((b, s, DIL_WIDTH), _bf16),
        jax.ShapeDtypeStruct((b, s, 2 * d), _bf16),
    ]
    out_specs = [
        pl.BlockSpec((1, MLA_HEADS, tm, HEAD_PAD), lambda bi, ti: (bi, 0, ti, 0)),
        pl.BlockSpec((1, MLA_HEADS, tm, HEAD_PAD), lambda bi, ti: (bi, 0, ti, 0)),
        pl.BlockSpec((1, MLA_HEADS, 1, MLA_V, tm), lambda bi, ti: (bi, 0, ti, 0, 0)),
        pl.BlockSpec((1, tm, MLA_WIDTH), lambda bi, ti: (bi, ti, 0)),
        *dil_specs,
        pl.BlockSpec((1, tm, DIL_WIDTH), lambda bi, ti: (bi, ti, 0)),
        pl.BlockSpec((1, tm, 2 * d), lambda bi, ti: (bi, ti, 0)),
    ]
    kern = functools.partial(_in_kernel, q_scale=(MLA_NOPE + MLA_ROPE) ** -0.5 * math.log2(math.e))
    return pl.pallas_call(
        kern,
        grid=(b, nt),
        in_specs=in_specs,
        out_specs=out_specs,
        out_shape=out_shape,
        scratch_shapes=[pltpu.VMEM((d // LANES, tm, LANES), _f32)],
        compiler_params=pltpu.CompilerParams(vmem_limit_bytes=VMEM_LIMIT),
        name="in_proj",
    )(x, scale, shift, lw["g_norm"], lw["w_big"], lw["g_cq"], lw["w_uq"], lw["g_ckv"],
      lw["w_k"], lw["w_vt"], lw["b_gate"], tab_m, *tabs_d)


def _mla_kernel(q_ref, k_ref, vt_ref, o_ref, sa_ref, sb_ref, m_ref, acc_ref, *, n_tiles, tq,
                n_chunks, tk, unroll):
    ones = jnp.ones((16, tk), _bf16)
    per = tk // vt_ref.shape[-1]
    bufs = (sa_ref, sb_ref)

    def scores(qi, c, dst_ref):
        q = q_ref[0, 0, pl.ds(pl.multiple_of(qi * tq, tq), tq), :]
        kc = k_ref[0, 0, pl.ds(pl.multiple_of(c * tk, tk), tk), :]
        st = _dot_nt(kc, q)
        dst_ref[...] = st
        return jnp.max(st, axis=0, keepdims=True)

    def consume(c, src_ref, cmax):
        m_old = m_ref[...]
        m_new = jnp.maximum(m_old, cmax)
        alpha = jnp.exp2(m_old - m_new)
        p = jnp.exp2(src_ref[...] - m_new).astype(_bf16)
        vc = jnp.concatenate([vt_ref[0, 0, c * per + j] for j in range(per)], axis=1)
        va = jnp.concatenate([vc, ones], axis=0)
        acc_ref[...] = alpha * acc_ref[...] + _dot(va, p)
        m_ref[...] = m_new

    def steps(qi, first, count, cm):
        for u in range(count):
            cm_next = scores(qi, first + u + 1, bufs[(u + 1) % 2])
            consume(first + u, bufs[u % 2], cm)
            cm = cm_next
        return cm

    def tile(qi, cm):
        m_ref[...] = jnp.full(m_ref.shape, NEG_INF, _f32)
        acc_ref[...] = jnp.zeros(acc_ref.shape, _f32)
        cm = lax.fori_loop(0, n_chunks // unroll - 1,
                           lambda j, cm: steps(qi, j * unroll, unroll, cm), cm)
        cm = steps(qi, n_chunks - unroll, unroll - 1, cm)
        cm_next = scores(jnp.minimum(qi + 1, n_tiles - 1), 0, bufs[0])
        consume(n_chunks - 1, bufs[1], cm)
        acc = acc_ref[...]
        o_ref[0, 0, qi] = (acc[:MLA_V] / acc[MLA_V:MLA_V + 1]).astype(_bf16)
        return cm_next

    lax.fori_loop(0, n_tiles, tile, scores(0, 0, sa_ref))


def _mla_call(q, k, vt):
    b, hh, s, _ = q.shape
    nv, tv = vt.shape[2], vt.shape[4]
    tk = min(MLA_K_CHUNK, s // 2)
    n_chunks = s // tk
    unroll = MLA_UNROLL if n_chunks >= 4 * MLA_UNROLL else 2
    assert n_chunks % unroll == 0 and unroll % 2 == 0 and tk % tv == 0
    tq = min(MLA_Q_TILE, s)
    n_tiles = s // tq
    kern = functools.partial(_mla_kernel, n_tiles=n_tiles, tq=tq, n_chunks=n_chunks, tk=tk,
                             unroll=unroll)
    return pl.pallas_call(
        kern,
        grid=(b, hh),
        in_specs=[
            pl.BlockSpec((1, 1, s, HEAD_PAD), lambda bi, hi: (bi, hi, 0, 0)),
            pl.BlockSpec((1, 1, s, HEAD_PAD), lambda bi, hi: (bi, hi, 0, 0)),
            pl.BlockSpec((1, 1, nv, MLA_V, tv), lambda bi, hi: (bi, hi, 0, 0, 0)),
        ],
        out_specs=pl.BlockSpec((1, 1, n_tiles, MLA_V, tq), lambda bi, hi: (bi, hi, 0, 0, 0)),
        out_shape=jax.ShapeDtypeStruct((b, hh, n_tiles, MLA_V, tq), _bf16),
        scratch_shapes=[pltpu.VMEM((tk, tq), _f32), pltpu.VMEM((tk, tq), _f32),
                        pltpu.VMEM((1, tq), _f32), pltpu.VMEM((MLA_V + 16, tq), _f32)],
        compiler_params=pltpu.CompilerParams(vmem_limit_bytes=VMEM_LIMIT),
        name="mla_attn",
    )(q, k, vt)


def _band_kernel(q_ref, kp_ref, kc_ref, kn_ref, vp_ref, vc_ref, vn_ref, o_ref, lse_ref,
                 *, dil, tq, seq):
    i = pl.program_id(1)
    nk = BAND_SUB + 2 * BAND_HALF
    row = lax.broadcasted_iota(jnp.int32, (BAND_SUB, nk), 0)
    col = lax.broadcasted_iota(jnp.int32, (BAND_SUB, nk), 1)
    band = jnp.abs(col - BAND_HALF - row) <= BAND_HALF
    lane = lax.broadcasted_iota(jnp.int32, (1, LANES), 1)
    first_head = (lane % 64) < 32
    low_half = lane < 64

    def residue(r, carry):
        q = q_ref[0, 0, r]
        kk = jnp.concatenate([kp_ref[0, 0, r], kc_ref[0, 0, r], kn_ref[0, 0, r]], axis=0)
        vv = jnp.concatenate([vp_ref[0, 0, r], vc_ref[0, 0, r], vn_ref[0, 0, r]], axis=0)
        for sb in range(tq // BAND_SUB):
            a = sb * BAND_SUB
            kidx = i * tq + (a - BAND_HALF) + col
            valid = band & (kidx >= 0) & (kidx < seq)
            dst = pl.ds(r + a * dil, BAND_SUB, stride=dil) if dil > 1 else pl.ds(a, BAND_SUB)
            lse_all = None
            for p in range(DIL_SLABS):
                sl = slice(p * LANES, (p + 1) * LANES)
                qp = q[a:a + BAND_SUB, sl]
                kp = kk[a:a + nk, sl]
                vp = vv[a:a + nk, sl]
                res = []
                for sel in (first_head, jnp.logical_not(first_head)):
                    qm = jnp.where(sel, qp, jnp.zeros_like(qp))
                    sc = jnp.where(valid, _dot_nt(qm, kp), NEG_INF)
                    m = jnp.max(sc, axis=-1, keepdims=True)
                    e = jnp.exp2(sc - m)
                    l = jnp.sum(e, axis=-1, keepdims=True)
                    o = _dot(e.astype(_bf16), vp) / l
                    res.append((o, m + jnp.log2(l)))
                o_ref[0, p, dst, :] = jnp.where(low_half, res[0][0], res[1][0])
                pair = jnp.where(lane % 32 < LSE_LANES, res[0][1], res[1][1])
                lse_all = pair if p == 0 else jnp.where(lane // 32 == p, pair, lse_all)
            lse_ref[0, dst, :] = lse_all
        return carry

    lax.fori_loop(0, dil, residue, 0, unroll=min(dil, 2))


def _band_call(qkv, dil):
    _, b, _, seq, w = qkv.shape
    s = seq * dil
    tq = max(BAND_SUB, BAND_ROWS // dil)
    assert tq % BAND_SUB == 0 and seq % tq == 0
    hb = tq // BAND_HALF
    last = seq // BAND_HALF - 1

    def cur(j):
        return pl.BlockSpec((1, 1, dil, tq, w), lambda bi, i: (j, bi, 0, i, 0))

    def prev(j):
        return pl.BlockSpec((1, 1, dil, BAND_HALF, w),
                            lambda bi, i: (j, bi, 0, jnp.maximum(i * hb - 1, 0), 0))

    def nxt(j):
        return pl.BlockSpec((1, 1, dil, BAND_HALF, w),
                            lambda bi, i: (j, bi, 0, jnp.minimum((i + 1) * hb, last), 0))

    kern = functools.partial(_band_kernel, dil=dil, tq=tq, seq=seq)
    return pl.pallas_call(
        kern,
        grid=(b, seq // tq),
        in_specs=[cur(0), prev(1), cur(1), nxt(1), prev(2), cur(2), nxt(2)],
        out_specs=[pl.BlockSpec((1, DIL_SLABS, dil * tq, LANES), lambda bi, i: (bi, 0, i, 0)),
                   pl.BlockSpec((1, dil * tq, LANES), lambda bi, i: (bi, i, 0))],
        out_shape=[jax.ShapeDtypeStruct((b, DIL_SLABS, s, LANES), _f32),
                   jax.ShapeDtypeStruct((b, s, LANES), _f32)],
        compiler_params=pltpu.CompilerParams(vmem_limit_bytes=VMEM_LIMIT),
        name=f"band_attn_d{dil}",
    )(qkv, qkv, qkv, qkv, qkv, qkv, qkv)


def _out_kernel(x_ref, ot_ref, zm_ref, o0_ref, o1_ref, o2_ref, l0_ref, l1_ref, l2_ref,
                zd_ref, gt_ref, gate_ref, wpa_ref, wpb_ref, wo_ref, gf_ref, ex_ref, y_ref,
                *, final_norm):
    d = x_ref.shape[-1]
    ot = jnp.concatenate([ot_ref[0, hd, 0] for hd in range(MLA_HEADS)], axis=0)
    o_mla = jnp.transpose(ot.astype(_f32)) * zm_ref[0].astype(_f32)
    a = _dot(o_mla.astype(_bf16), wpa_ref[...])

    l0, l1, l2 = l0_ref[0], l1_ref[0], l2_ref[0]
    mx = jnp.maximum(jnp.maximum(l0, l1), l2)
    w0, w1, w2 = jnp.exp2(l0 - mx), jnp.exp2(l1 - mx), jnp.exp2(l2 - mx)
    inv = 1.0 / (w0 + w1 + w2)
    o_dil = None
    for w, o_ref in ((w0, o0_ref), (w1, o1_ref), (w2, o2_ref)):
        wide = _dot((w * inv).astype(_bf16), ex_ref[...])
        term = wide * jnp.concatenate([o_ref[0, p] for p in range(DIL_SLABS)], axis=1)
        o_dil = term if o_dil is None else o_dil + term
    o_dil = o_dil * zd_ref[0].astype(_f32)
    bb = _dot(o_dil.astype(_bf16), wpb_ref[...])

    gt = gt_ref[0].astype(_f32)
    u = gt[:, :d] * a + gt[:, d:] * bb
    y = x_ref[0] + gate_ref[0] * _dot(u.astype(_bf16), wo_ref[...])
    if final_norm:
        y = _rms(y, gf_ref[...])
    y_ref[0] = y


def _out_call(x, ot, zm, outs, lses, zd, gt, gate, lw, g_final, final_norm):
    b, s, d = x.shape
    tm = min(TOKEN_TILE, s)
    tok = lambda w: pl.BlockSpec((1, tm, w), lambda bi, ti: (bi, ti, 0))
    slab = pl.BlockSpec((1, DIL_SLABS, tm, LANES), lambda bi, ti: (bi, 0, ti, 0))
    const = lambda *shape: pl.BlockSpec(shape, lambda bi, ti: (0,) * len(shape))
    kern = functools.partial(_out_kernel, final_norm=final_norm)
    assert ot.shape[-1] % tm == 0
    per = ot.shape[-1] // tm
    spread = np.zeros((LANES, DIL_WIDTH), np.float32)
    for hd in range(DIL_HEADS):
        spread[hd * LSE_LANES, hd * DIL_HEAD_DIM:(hd + 1) * DIL_HEAD_DIM] = 1.0
    return pl.pallas_call(
        kern,
        grid=(b, s // tm),
        in_specs=[tok(d),
                  pl.BlockSpec((1, MLA_HEADS, 1, MLA_V, tm),
                               lambda bi, ti: (bi, 0, ti // per, 0, ti % per)),
                  tok(MLA_WIDTH),
                  slab, slab, slab, tok(LANES), tok(LANES), tok(LANES),
                  tok(DIL_WIDTH), tok(2 * d),
                  pl.BlockSpec((1, 1, d), lambda bi, ti: (bi, 0, 0)),
                  const(MLA_WIDTH, d), const(DIL_WIDTH, d), const(d, d), const(1, d),
                  const(LANES, DIL_WIDTH)],
        out_specs=tok(d),
        out_shape=jax.ShapeDtypeStruct((b, s, d), _f32),
        compiler_params=pltpu.CompilerParams(vmem_limit_bytes=VMEM_LIMIT),
        name="out_proj",
    )(x, ot, zm, outs[0], outs[1], outs[2], lses[0], lses[1], lses[2],
      zd, gt, gate, lw["w_pa"], lw["w_pb"], lw["w_out"], g_final, jnp.asarray(spread, _bf16))


def _mla_lane_order():
    half = MLA_ROPE // 2
    src = np.full((HEAD_PAD,), -1, np.int64)
    src[0:half] = MLA_NOPE + np.arange(half)
    src[half:64] = np.arange(64 - half)
    src[64:64 + half] = MLA_NOPE + half + np.arange(half)
    src[64 + half:64 + half + (MLA_NOPE - (64 - half))] = np.arange(64 - half, MLA_NOPE)
    return src


def _gather_cols(w, src):
    picked = jnp.take(w, jnp.asarray(np.maximum(src, 0)), axis=1)
    return jnp.where(jnp.asarray(src >= 0)[None, :], picked, 0.0)


def _dil_perm():
    idx = []
    half = DIL_HEAD_DIM // 2
    for p in range(DIL_HEADS // 2):
        ha, hb = 2 * p * DIL_HEAD_DIM, (2 * p + 1) * DIL_HEAD_DIM
        idx += list(range(ha, ha + half)) + list(range(hb, hb + half))
        idx += list(range(ha + half, ha + 2 * half)) + list(range(hb + half, hb + 2 * half))
    return np.asarray(idx, np.int64)


def _pack_layer(w_in, g_norm, b_gate, g_cq, w_uq, g_ckv, w_ukv, w_pa, w_pb, w_out):
    d = w_in.shape[0]
    splits = np.cumsum([Q_LORA, KV_LORA, MLA_ROPE, MLA_WIDTH] + [DIL_WIDTH] * (3 * N_GROUPS)
                       + [DIL_WIDTH])
    parts = jnp.split(w_in, list(splits), axis=1)
    perm = _dil_perm()
    lane_src = _mla_lane_order()
    kr_src = np.where(lane_src >= MLA_NOPE, lane_src - MLA_NOPE, -1)
    cols = [parts[0], parts[1], _gather_cols(parts[2], kr_src), parts[3]]
    for j in range(3 * N_GROUPS):
        pj = parts[4 + j]
        cols.append(pj if j % 3 == 2 else jnp.take(pj, jnp.asarray(perm), axis=1))
    cols += [parts[4 + 3 * N_GROUPS], parts[5 + 3 * N_GROUPS]]
    w_big = jnp.concatenate(cols, axis=1).astype(_bf16)

    dq = MLA_NOPE + MLA_ROPE
    q_src = np.concatenate([np.where(lane_src >= 0, h * dq + lane_src, -1) for h in range(MLA_HEADS)])
    w_uq_p = _gather_cols(w_uq, q_src).astype(_bf16)
    dkv = MLA_NOPE + MLA_V
    nope_src = np.where((lane_src >= 0) & (lane_src < MLA_NOPE), lane_src, -1)
    k_src = np.concatenate([np.where(nope_src >= 0, h * dkv + nope_src, -1) for h in range(MLA_HEADS)])
    w_k = _gather_cols(w_ukv, k_src).astype(_bf16)
    v_src = np.concatenate([h * dkv + MLA_NOPE + np.arange(MLA_V) for h in range(MLA_HEADS)])
    w_vt = jnp.transpose(jnp.take(w_ukv, jnp.asarray(v_src), axis=1)).astype(_bf16)

    return dict(
        w_big=w_big, g_norm=g_norm.reshape(1, d), b_gate=b_gate.reshape(1, -1),
        g_cq=g_cq.reshape(1, -1), w_uq=w_uq_p, g_ckv=g_ckv.reshape(1, -1), w_k=w_k,
        w_vt=w_vt,
        w_pa=w_pa.astype(_bf16), w_pb=w_pb.astype(_bf16), w_out=w_out.astype(_bf16))


def _rope_tables(s, tm):
    pos = np.arange(s, dtype=np.float64)[:, None]
    lane = np.arange(LANES)

    def table(freq_idx, active, n_freq):
        inv = np.power(ROPE_THETA, -2.0 * freq_idx / (2 * n_freq))
        ang = pos * inv[None, :]
        cos = np.where(active[None, :], np.cos(ang), 1.0)
        sign = np.where(lane < 64, -1.0, 1.0)
        sin = np.where(active[None, :], np.sin(ang) * sign[None, :], 0.0)
        return np.stack([cos, sin]).astype(np.float32)

    half = MLA_ROPE // 2
    tab_m = table((lane % 64).clip(0, half - 1).astype(np.float64), (lane % 64) < half, half)
    tab_d = jnp.asarray(table((lane % 32).astype(np.float64), np.ones(LANES, bool), DIL_HEAD_DIM // 2))
    tabs = []
    for _, dil in DIL_GROUPS:
        t = tab_d.reshape(2, s // tm, tm // dil, dil, LANES)
        tabs.append(jnp.swapaxes(t, 2, 3).reshape(2, s, LANES))
    return jnp.asarray(tab_m), tabs


def _trunk(x, mods, layers, g_final):
    b, s, d = x.shape
    assert s % (DIL_GROUPS[-1][1] * BAND_SUB) == 0 and s % TOKEN_TILE == 0
    tab_m, tabs_d = _rope_tables(s, min(TOKEN_TILE, s))
    depth = len(layers)
    for l, lw in enumerate(layers):
        mod = mods[l]
        shift, scale, gate = [mod[:, None, j * d:(j + 1) * d] for j in range(3)]
        q, k, vt, zm, d1, d4, d16, zd, gt = _in_call(x, scale, shift, lw, tab_m, tabs_d)
        ot = _mla_call(q, k, vt)
        outs, lses = [], []
        for qkv, (_, dl) in zip((d1, d4, d16), DIL_GROUPS):
            o_g, lse_g = _band_call(qkv, dl)
            outs.append(o_g)
            lses.append(lse_g)
        x = _out_call(x, ot, zm, outs, lses, zd, gt, gate, lw, g_final.reshape(1, d),
                      final_norm=(l == depth - 1))
    return x


def kernel(x_prompt, x_sample, c_prompt, c_sample, w_ada, b_ada, g_norm, w_in, b_gate, g_cq, w_uq,
           g_ckv, w_ukv, w_pa, w_pb, w_out, g_final):
    depth = w_in.shape[0]
    bp, bs = c_prompt.shape[0], c_sample.shape[0]
    rows = -(-(bp + bs) // 8) * 8
    c_all = jnp.concatenate([c_prompt, c_sample], axis=0)
    c_all = jnp.pad(c_all, ((0, rows - bp - bs), (0, 0)))
    mods = _ada_call(c_all, w_ada, b_ada)
    layers = [_pack_layer(w_in[l], g_norm[l], b_gate[l], g_cq[l], w_uq[l], g_ckv[l], w_ukv[l],
                          w_pa[l], w_pb[l], w_out[l]) for l in range(depth)]
    y_prompt = _trunk(x_prompt, mods[:, :bp], layers, g_final)
    y_sample = _trunk(x_sample, mods[:, bp:bp + bs], layers, g_final)
    return (y_prompt, y_sample)
```

```python
import functools
import math

import numpy as np
import jax
import jax.numpy as jnp
from jax import lax
from jax.experimental import pallas as pl
from jax.experimental.pallas import tpu as pltpu

ROPE_THETA = 10000.0
EPS = 1e-6
NEG_INF = -1e30

MLA_HEADS = 8
MLA_NOPE = 64
MLA_ROPE = 32
MLA_V = 64
Q_LORA = 384
KV_LORA = 256
MLA_WIDTH = MLA_HEADS * MLA_V
DIL_GROUPS = ((128, 1), (512, 4), (2048, 16))
DIL_HEADS = 8
DIL_HEAD_DIM = 64
DIL_WIDTH = DIL_HEADS * DIL_HEAD_DIM
N_GROUPS = len(DIL_GROUPS)
BAND_HALF = 64

LANES = 128
KR_PAD = LANES
HEAD_PAD = LANES
DIL_SLABS = DIL_WIDTH // LANES
LSE_LANES = LANES // DIL_HEADS

OFF_CQ = 0
OFF_CKV = OFF_CQ + Q_LORA
OFF_KR = OFF_CKV + KV_LORA
OFF_ZM = OFF_KR + KR_PAD
OFF_DIL = OFF_ZM + MLA_WIDTH
OFF_ZD = OFF_DIL + 3 * N_GROUPS * DIL_WIDTH
OFF_MG = OFF_ZD + DIL_WIDTH
W_BIG = OFF_MG + 2 * 1024

TOKEN_TILE = 512
MLA_Q_TILE = 1024
MLA_K_CHUNK = 512
MLA_UNROLL = 8
BAND_SUB = 128
BAND_ROWS = 512
VMEM_LIMIT = 56 * 1024 * 1024

_f32 = jnp.float32
_bf16 = jnp.bfloat16
_NT = (((1,), (1,)), ((), ()))


def _dot(a, b):
    return jnp.dot(a, b, preferred_element_type=_f32)


def _dot_nt(a, b):
    return lax.dot_general(a, b, _NT, preferred_element_type=_f32)


def _rms(x, g):
    return x * lax.rsqrt(jnp.mean(x * x, axis=-1, keepdims=True) + EPS) * g


def _sigmoid(x):
    return 1.0 / (1.0 + jnp.exp(-x))


def _rope_pairs(x, cos, sin):
    return x * cos + pltpu.roll(x, 64, axis=1) * sin


def _ada_kernel(c_ref, w_ref, b_ref, o_ref):
    c = c_ref[...]
    a = c * _sigmoid(c)
    a_hi = a.astype(_bf16)
    a_lo = (a - a_hi.astype(_f32)).astype(_bf16)
    w = w_ref[0]
    w_hi = w.astype(_bf16)
    w_lo = (w - w_hi.astype(_f32)).astype(_bf16)
    o_ref[0] = _dot(a_hi, w_hi) + _dot(a_hi, w_lo) + _dot(a_lo, w_hi) + b_ref[0]


def _ada_call(c_all, w_ada, b_ada):
    depth, d, n = w_ada.shape
    rows = c_all.shape[0]
    nb = n // d
    return pl.pallas_call(
        _ada_kernel,
        grid=(depth, nb),
        in_specs=[
            pl.BlockSpec((rows, d), lambda l, j: (0, 0)),
            pl.BlockSpec((1, d, d), lambda l, j: (l, 0, j)),
            pl.BlockSpec((1, 1, d), lambda l, j: (l, 0, j)),
        ],
        out_specs=pl.BlockSpec((1, rows, d), lambda l, j: (l, 0, j)),
        out_shape=jax.ShapeDtypeStruct((depth, rows, n), _f32),
        compiler_params=pltpu.CompilerParams(vmem_limit_bytes=VMEM_LIMIT),
        name="adaln",
    )(c_all, w_ada, b_ada.reshape(depth, 1, n))


def _in_kernel(x_ref, sc_ref, sh_ref, gn_ref, w_ref, gcq_ref, wuq_ref, gckv_ref, wk_ref,
               wvt_ref, bg_ref, tm_ref, t1_ref, t4_ref, t16_ref,
               q_ref, k_ref, vt_ref, zm_ref, d1_ref, d4_ref, d16_ref, zd_ref, gt_ref,
               hs_ref, *, q_scale):
    x = x_ref[0]
    rows, d_model = x.shape
    h = _rms(x, gn_ref[...]) * (1.0 + sc_ref[0]) + sh_ref[0]
    hb = h.astype(_bf16)
    for c in range(d_model // LANES):
        hs_ref[c] = h[:, c * LANES:(c + 1) * LANES]
    cos_m, sin_m = tm_ref[0], tm_ref[1]

    cq = _dot(hb, w_ref[:, OFF_CQ:OFF_CQ + Q_LORA])
    cqn = _rms(cq, gcq_ref[...]).astype(_bf16)
    q = _dot(cqn, wuq_ref[...])
    for hd in range(MLA_HEADS):
        sl = slice(hd * HEAD_PAD, (hd + 1) * HEAD_PAD)
        q_ref[0, hd] = (_rope_pairs(q[:, sl], cos_m, sin_m) * q_scale).astype(_bf16)

    ckv = _dot(hb, w_ref[:, OFF_CKV:OFF_CKV + KV_LORA])
    ckvn = _rms(ckv, gckv_ref[...]).astype(_bf16)
    kr = _dot(hb, w_ref[:, OFF_KR:OFF_KR + KR_PAD])
    k = _dot(ckvn, wk_ref[...])
    for hd in range(MLA_HEADS):
        sl = slice(hd * HEAD_PAD, (hd + 1) * HEAD_PAD)
        k_ref[0, hd] = _rope_pairs(k[:, sl] + kr, cos_m, sin_m).astype(_bf16)
    vt = _dot_nt(wvt_ref[...], ckvn)
    for hd in range(MLA_HEADS):
        vt_ref[0, hd, 0] = vt[hd * MLA_V:(hd + 1) * MLA_V, :].astype(_bf16)

    zm = _dot(hb, w_ref[:, OFF_ZM:OFF_ZM + MLA_WIDTH])
    zm_ref[0] = (zm * _sigmoid(zm)).astype(_bf16)

    zd = _dot(hb, w_ref[:, OFF_ZD:OFF_ZD + DIL_WIDTH])
    zd_ref[0] = (zd * _sigmoid(zd)).astype(_bf16)

    mg = _dot(hb, w_ref[:, OFF_MG:W_BIG]) + bg_ref[...]
    gt_ref[0] = _sigmoid(mg).astype(_bf16)

    for g, (dil, out_ref, tab_ref) in enumerate(((1, d1_ref, t1_ref), (4, d4_ref, t4_ref),
                                                 (16, d16_ref, t16_ref))):
        per = rows // dil
        if dil == 1:
            hg = hb
        else:
            hg = jnp.concatenate(
                [jnp.concatenate([hs_ref[c, pl.ds(r, per, stride=dil), :] for r in range(dil)], axis=0)
                 for c in range(d_model // LANES)], axis=1).astype(_bf16)
        cos_d, sin_d = tab_ref[0], tab_ref[1]
        for kind in range(3):
            off = OFF_DIL + (3 * g + kind) * DIL_WIDTH
            u = _dot(hg, w_ref[:, off:off + DIL_WIDTH])
            if kind == 2:
                ub = u.astype(_bf16)
            else:
                post = DIL_HEAD_DIM ** -0.5 * math.log2(math.e) if kind == 0 else 1.0
                ub = jnp.concatenate(
                    [(_rope_pairs(u[:, p * LANES:(p + 1) * LANES], cos_d, sin_d) * post).astype(_bf16)
                     for p in range(DIL_SLABS)], axis=1)
            for r in range(dil):
                out_ref[kind, 0, r] = ub[r * per:(r + 1) * per]


def _in_call(x, scale, shift, lw, tab_m, tabs_d):
    b, s, d = x.shape
    tm = min(TOKEN_TILE, s)
    nt = s // tm
    const = lambda *shape: pl.BlockSpec(shape, lambda bi, ti: (0,) * len(shape),
                                        pipeline_mode=pl.Buffered(1))
    tab = pl.BlockSpec((2, tm, LANES), lambda bi, ti: (0, ti, 0))
    in_specs = [
        pl.BlockSpec((1, tm, d), lambda bi, ti: (bi, ti, 0)),
        pl.BlockSpec((1, 1, d), lambda bi, ti: (bi, 0, 0)),
        pl.BlockSpec((1, 1, d), lambda bi, ti: (bi, 0, 0)),
        const(1, d),
        const(d, W_BIG),
        const(1, Q_LORA),
        const(Q_LORA, MLA_HEADS * HEAD_PAD),
        const(1, KV_LORA),
        const(KV_LORA, MLA_HEADS * HEAD_PAD),
        const(MLA_WIDTH, KV_LORA),
        const(1, 2 * d),
        tab, tab, tab, tab,
    ]
    dil_shapes = [jax.ShapeDtypeStruct((3, b, dl, s // dl, DIL_WIDTH), _bf16) for _, dl in DIL_GROUPS]
    dil_specs = [pl.BlockSpec((3, 1, dl, tm // dl, DIL_WIDTH), lambda bi, ti: (0, bi, 0, ti, 0))
                 for _, dl in DIL_GROUPS]
    out_shape = [
        jax.ShapeDtypeStruct((b, MLA_HEADS, s, HEAD_PAD), _bf16),
        jax.ShapeDtypeStruct((b, MLA_HEADS, s, HEAD_PAD), _bf16),
        jax.ShapeDtypeStruct((b, MLA_HEADS, nt, MLA_V, tm), _bf16),
        jax.ShapeDtypeStruct((b, s, MLA_WIDTH), _bf16),
        *dil_shapes,
        jax.ShapeDtypeStruct((b, s, DIL_WIDTH), _bf16),
        jax.ShapeDtypeStruct((b, s, 2 * d), _bf16),
    ]
    out_specs = [
        pl.BlockSpec((1, MLA_HEADS, tm, HEAD_PAD), lambda bi, ti: (bi, 0, ti, 0)),
        pl.BlockSpec((1, MLA_HEADS, tm, HEAD_PAD), lambda bi, ti: (bi, 0, ti, 0)),
        pl.BlockSpec((1, MLA_HEADS, 1, MLA_V, tm), lambda bi, ti: (bi, 0, ti, 0, 0)),
        pl.BlockSpec((1, tm, MLA_WIDTH), lambda bi, ti: (bi, ti, 0)),
        *dil_specs,
        pl.BlockSpec((1, tm, DIL_WIDTH), lambda bi, ti: (bi, ti, 0)),
        pl.BlockSpec((1, tm, 2 * d), lambda bi, ti: (bi, ti, 0)),
    ]
    kern = functools.partial(_in_kernel, q_scale=(MLA_NOPE + MLA_ROPE) ** -0.5 * math.log2(math.e))
    return pl.pallas_call(
        kern,
        grid=(b, nt),
        in_specs=in_specs,
        out_specs=out_specs,
        out_shape=out_shape,
        scratch_shapes=[pltpu.VMEM((d // LANES, tm, LANES), _f32)],
        compiler_params=pltpu.CompilerParams(vmem_limit_bytes=VMEM_LIMIT),
        name="in_proj",
    )(x, scale, shift, lw["g_norm"], lw["w_big"], lw["g_cq"], lw["w_uq"], lw["g_ckv"],
      lw["w_k"], lw["w_vt"], lw["b_gate"], tab_m, *tabs_d)


def _mla_kernel(q_ref, k_ref, vt_ref, o_ref, sa_ref, sb_ref, m_ref, acc_ref, *, n_tiles, tq,
                n_chunks, tk, unroll):
    ones = jnp.ones((16, tk), _bf16)
    per = tk // vt_ref.shape[-1]
    bufs = (sa_ref, sb_ref)

    def scores(qi, c, dst_ref):
        q = q_ref[0, 0, pl.ds(pl.multiple_of(qi * tq, tq), tq), :]
        kc = k_ref[0, 0, pl.ds(pl.multiple_of(c * tk, tk), tk), :]
        st = _dot_nt(kc, q)
        dst_ref[...] = st
        return jnp.max(st, axis=0, keepdims=True)

    def consume(c, src_ref, cmax):
        m_old = m_ref[...]
        m_new = jnp.maximum(m_old, cmax)
        alpha = jnp.exp2(m_old - m_new)
        p = jnp.exp2(src_ref[...] - m_new).astype(_bf16)
        vc = jnp.concatenate([vt_ref[0, 0, c * per + j] for j in range(per)], axis=1)
        va = jnp.concatenate([vc, ones], axis=0)
        acc_ref[...] = alpha * acc_ref[...] + _dot(va, p)
        m_ref[...] = m_new

    def steps(qi, first, count, cm):
        for u in range(count):
            cm_next = scores(qi, first + u + 1, bufs[(u + 1) % 2])
            consume(first + u, bufs[u % 2], cm)
            cm = cm_next
        return cm

    def tile(qi, cm):
        m_ref[...] = jnp.full(m_ref.shape, NEG_INF, _f32)
        acc_ref[...] = jnp.zeros(acc_ref.shape, _f32)
        cm = lax.fori_loop(0, n_chunks // unroll - 1,
                           lambda j, cm: steps(qi, j * unroll, unroll, cm), cm)
        cm = steps(qi, n_chunks - unroll, unroll - 1, cm)
        cm_next = scores(jnp.minimum(qi + 1, n_tiles - 1), 0, bufs[0])
        consume(n_chunks - 1, bufs[1], cm)
        acc = acc_ref[...]
        o_ref[0, 0, qi] = (acc[:MLA_V] / acc[MLA_V:MLA_V + 1]).astype(_bf16)
        return cm_next

    lax.fori_loop(0, n_tiles, tile, scores(0, 0, sa_ref))


def _mla_call(q, k, vt):
    b, hh, s, _ = q.shape
    nv, tv = vt.shape[2], vt.shape[4]
    tk = min(MLA_K_CHUNK, s // 2)
    n_chunks = s // tk
    unroll = MLA_UNROLL if n_chunks >= 4 * MLA_UNROLL else 2
    assert n_chunks % unroll == 0 and unroll % 2 == 0 and tk % tv == 0
    tq = min(MLA_Q_TILE, s)
    n_tiles = s // tq
    kern = functools.partial(_mla_kernel, n_tiles=n_tiles, tq=tq, n_chunks=n_chunks, tk=tk,
                             unroll=unroll)
    return pl.pallas_call(
        kern,
        grid=(b, hh),
        in_specs=[
            pl.BlockSpec((1, 1, s, HEAD_PAD), lambda bi, hi: (bi, hi, 0, 0)),
            pl.BlockSpec((1, 1, s, HEAD_PAD), lambda bi, hi: (bi, hi, 0, 0)),
            pl.BlockSpec((1, 1, nv, MLA_V, tv), lambda bi, hi: (bi, hi, 0, 0, 0)),
        ],
        out_specs=pl.BlockSpec((1, 1, n_tiles, MLA_V, tq), lambda bi, hi: (bi, hi, 0, 0, 0)),
        out_shape=jax.ShapeDtypeStruct((b, hh, n_tiles, MLA_V, tq), _bf16),
        scratch_shapes=[pltpu.VMEM((tk, tq), _f32), pltpu.VMEM((tk, tq), _f32),
                        pltpu.VMEM((1, tq), _f32), pltpu.VMEM((MLA_V + 16, tq), _f32)],
        compiler_params=pltpu.CompilerParams(vmem_limit_bytes=VMEM_LIMIT),
        name="mla_attn",
    )(q, k, vt)


def _band_kernel(q_ref, kp_ref, kc_ref, kn_ref, vp_ref, vc_ref, vn_ref, o_ref, stat_ref,
                 *, dil, tq, seq):
    i = pl.program_id(1)
    nk = BAND_SUB + 2 * BAND_HALF
    row = lax.broadcasted_iota(jnp.int32, (BAND_SUB, nk), 0)
    col = lax.broadcasted_iota(jnp.int32, (BAND_SUB, nk), 1)
    band = jnp.abs(col - BAND_HALF - row) <= BAND_HALF
    lane = lax.broadcasted_iota(jnp.int32, (1, LANES), 1)
    first_head = (lane % 64) < 32
    low_half = lane < 64

    def residue(r, carry):
        q = q_ref[0, 0, r]
        kk = jnp.concatenate([kp_ref[0, 0, r], kc_ref[0, 0, r], kn_ref[0, 0, r]], axis=0)
        vv = jnp.concatenate([vp_ref[0, 0, r], vc_ref[0, 0, r], vn_ref[0, 0, r]], axis=0)
        for sb in range(tq // BAND_SUB):
            a = sb * BAND_SUB
            if 0 < sb < tq // BAND_SUB - 1:
                valid = band
            else:
                kidx = i * tq + (a - BAND_HALF) + col
                valid = band & (kidx >= 0) & (kidx < seq)
            dst = pl.ds(r + a * dil, BAND_SUB, stride=dil) if dil > 1 else pl.ds(a, BAND_SUB)
            stats = [None, None]
            for p in range(DIL_SLABS):
                sl = slice(p * LANES, (p + 1) * LANES)
                qp = q[a:a + BAND_SUB, sl]
                kp = kk[a:a + nk, sl]
                vp = vv[a:a + nk, sl]
                res = []
                for sel in (first_head, jnp.logical_not(first_head)):
                    qm = jnp.where(sel, qp, jnp.zeros_like(qp))
                    sc = jnp.where(valid, _dot_nt(qm, kp), NEG_INF)
                    m = jnp.max(sc, axis=-1, keepdims=True)
                    e = jnp.exp2(sc - m)
                    l = jnp.sum(e, axis=-1, keepdims=True)
                    o = _dot(e.astype(_bf16), vp) / l
                    res.append((o, m, l))
                o_ref[0, p, dst, :] = jnp.where(low_half, res[0][0], res[1][0])
                even = lane % 32 < LSE_LANES
                here = lane // 32 == p
                for j in range(2):
                    pair = jnp.where(even, res[0][1 + j], res[1][1 + j])
                    stats[j] = pair if p == 0 else jnp.where(here, pair, stats[j])
            stat_ref[0, 0, dst, :] = stats[0]
            stat_ref[0, 1, dst, :] = stats[1]
        return carry

    lax.fori_loop(0, dil, residue, 0, unroll=min(dil, 2))


def _band_call(qkv, dil):
    _, b, _, seq, w = qkv.shape
    s = seq * dil
    tq = max(BAND_SUB, BAND_ROWS // dil)
    assert tq % BAND_SUB == 0 and seq % tq == 0
    hb = tq // BAND_HALF
    last = seq // BAND_HALF - 1

    def cur(j):
        return pl.BlockSpec((1, 1, dil, tq, w), lambda bi, i: (j, bi, 0, i, 0))

    def prev(j):
        return pl.BlockSpec((1, 1, dil, BAND_HALF, w),
                            lambda bi, i: (j, bi, 0, jnp.maximum(i * hb - 1, 0), 0))

    def nxt(j):
        return pl.BlockSpec((1, 1, dil, BAND_HALF, w),
                            lambda bi, i: (j, bi, 0, jnp.minimum((i + 1) * hb, last), 0))

    kern = functools.partial(_band_kernel, dil=dil, tq=tq, seq=seq)
    return pl.pallas_call(
        kern,
        grid=(b, seq // tq),
        in_specs=[cur(0), prev(1), cur(1), nxt(1), prev(2), cur(2), nxt(2)],
        out_specs=[pl.BlockSpec((1, DIL_SLABS, dil * tq, LANES), lambda bi, i: (bi, 0, i, 0)),
                   pl.BlockSpec((1, 2, dil * tq, LANES), lambda bi, i: (bi, 0, i, 0))],
        out_shape=[jax.ShapeDtypeStruct((b, DIL_SLABS, s, LANES), _f32),
                   jax.ShapeDtypeStruct((b, 2, s, LANES), _f32)],
        compiler_params=pltpu.CompilerParams(vmem_limit_bytes=VMEM_LIMIT),
        name=f"band_attn_d{dil}",
    )(qkv, qkv, qkv, qkv, qkv, qkv, qkv)


def _out_kernel(x_ref, ot_ref, zm_ref, o0_ref, o1_ref, o2_ref, l0_ref, l1_ref, l2_ref,
                zd_ref, gt_ref, gate_ref, wpa_ref, wpb_ref, wo_ref, gf_ref, ex_ref, y_ref,
                *, final_norm):
    d = x_ref.shape[-1]
    ot = jnp.concatenate([ot_ref[0, hd, 0] for hd in range(MLA_HEADS)], axis=0)
    o_mla = jnp.transpose(ot.astype(_f32)) * zm_ref[0].astype(_f32)
    a = _dot(o_mla.astype(_bf16), wpa_ref[...])

    m0, m1, m2 = l0_ref[0, 0], l1_ref[0, 0], l2_ref[0, 0]
    mx = jnp.maximum(jnp.maximum(m0, m1), m2)
    w0 = l0_ref[0, 1] * jnp.exp2(m0 - mx)
    w1 = l1_ref[0, 1] * jnp.exp2(m1 - mx)
    w2 = l2_ref[0, 1] * jnp.exp2(m2 - mx)
    inv = 1.0 / (w0 + w1 + w2)
    o_dil = None
    for w, o_ref in ((w0, o0_ref), (w1, o1_ref), (w2, o2_ref)):
        wide = _dot((w * inv).astype(_bf16), ex_ref[...])
        term = wide * jnp.concatenate([o_ref[0, p] for p in range(DIL_SLABS)], axis=1)
        o_dil = term if o_dil is None else o_dil + term
    o_dil = o_dil * zd_ref[0].astype(_f32)
    bb = _dot(o_dil.astype(_bf16), wpb_ref[...])

    gt = gt_ref[0].astype(_f32)
    u = gt[:, :d] * a + gt[:, d:] * bb
    y = x_ref[0] + gate_ref[0] * _dot(u.astype(_bf16), wo_ref[...])
    if final_norm:
        y = _rms(y, gf_ref[...])
    y_ref[0] = y


def _out_call(x, ot, zm, outs, lses, zd, gt, gate, lw, g_final, final_norm):
    b, s, d = x.shape
    tm = min(TOKEN_TILE, s)
    tok = lambda w: pl.BlockSpec((1, tm, w), lambda bi, ti: (bi, ti, 0))
    slab = pl.BlockSpec((1, DIL_SLABS, tm, LANES), lambda bi, ti: (bi, 0, ti, 0))
    stat = pl.BlockSpec((1, 2, tm, LANES), lambda bi, ti: (bi, 0, ti, 0))
    const = lambda *shape: pl.BlockSpec(shape, lambda bi, ti: (0,) * len(shape))
    kern = functools.partial(_out_kernel, final_norm=final_norm)
    assert ot.shape[-1] % tm == 0
    per = ot.shape[-1] // tm
    spread = np.zeros((LANES, DIL_WIDTH), np.float32)
    for hd in range(DIL_HEADS):
        spread[hd * LSE_LANES, hd * DIL_HEAD_DIM:(hd + 1) * DIL_HEAD_DIM] = 1.0
    return pl.pallas_call(
        kern,
        grid=(b, s // tm),
        in_specs=[tok(d),
                  pl.BlockSpec((1, MLA_HEADS, 1, MLA_V, tm),
                               lambda bi, ti: (bi, 0, ti // per, 0, ti % per)),
                  tok(MLA_WIDTH),
                  slab, slab, slab, stat, stat, stat,
                  tok(DIL_WIDTH), tok(2 * d),
                  pl.BlockSpec((1, 1, d), lambda bi, ti: (bi, 0, 0)),
                  const(MLA_WIDTH, d), const(DIL_WIDTH, d), const(d, d), const(1, d),
                  const(LANES, DIL_WIDTH)],
        out_specs=tok(d),
        out_shape=jax.ShapeDtypeStruct((b, s, d), _f32),
        compiler_params=pltpu.CompilerParams(vmem_limit_bytes=VMEM_LIMIT),
        name="out_proj",
    )(x, ot, zm, outs[0], outs[1], outs[2], lses[0], lses[1], lses[2],
      zd, gt, gate, lw["w_pa"], lw["w_pb"], lw["w_out"], g_final, jnp.asarray(spread, _bf16))


def _mla_lane_order():
    half = MLA_ROPE // 2
    src = np.full((HEAD_PAD,), -1, np.int64)
    src[0:half] = MLA_NOPE + np.arange(half)
    src[half:64] = np.arange(64 - half)
    src[64:64 + half] = MLA_NOPE + half + np.arange(half)
    src[64 + half:64 + half + (MLA_NOPE - (64 - half))] = np.arange(64 - half, MLA_NOPE)
    return src


def _gather_cols(w, src):
    picked = jnp.take(w, jnp.asarray(np.maximum(src, 0)), axis=1)
    return jnp.where(jnp.asarray(src >= 0)[None, :], picked, 0.0)


def _dil_perm():
    idx = []
    half = DIL_HEAD_DIM // 2
    for p in range(DIL_HEADS // 2):
        ha, hb = 2 * p * DIL_HEAD_DIM, (2 * p + 1) * DIL_HEAD_DIM
        idx += list(range(ha, ha + half)) + list(range(hb, hb + half))
        idx += list(range(ha + half, ha + 2 * half)) + list(range(hb + half, hb + 2 * half))
    return np.asarray(idx, np.int64)


def _pack_layer(w_in, g_norm, b_gate, g_cq, w_uq, g_ckv, w_ukv, w_pa, w_pb, w_out):
    d = w_in.shape[0]
    splits = np.cumsum([Q_LORA, KV_LORA, MLA_ROPE, MLA_WIDTH] + [DIL_WIDTH] * (3 * N_GROUPS)
                       + [DIL_WIDTH])
    parts = jnp.split(w_in, list(splits), axis=1)
    perm = _dil_perm()
    lane_src = _mla_lane_order()
    kr_src = np.where(lane_src >= MLA_NOPE, lane_src - MLA_NOPE, -1)
    cols = [parts[0], parts[1], _gather_cols(parts[2], kr_src), parts[3]]
    for j in range(3 * N_GROUPS):
        pj = parts[4 + j]
        cols.append(pj if j % 3 == 2 else jnp.take(pj, jnp.asarray(perm), axis=1))
    cols += [parts[4 + 3 * N_GROUPS], parts[5 + 3 * N_GROUPS]]
    w_big = jnp.concatenate(cols, axis=1).astype(_bf16)

    dq = MLA_NOPE + MLA_ROPE
    q_src = np.concatenate([np.where(lane_src >= 0, h * dq + lane_src, -1) for h in range(MLA_HEADS)])
    w_uq_p = _gather_cols(w_uq, q_src).astype(_bf16)
    dkv = MLA_NOPE + MLA_V
    nope_src = np.where((lane_src >= 0) & (lane_src < MLA_NOPE), lane_src, -1)
    k_src = np.concatenate([np.where(nope_src >= 0, h * dkv + nope_src, -1) for h in range(MLA_HEADS)])
    w_k = _gather_cols(w_ukv, k_src).astype(_bf16)
    v_src = np.concatenate([h * dkv + MLA_NOPE + np.arange(MLA_V) for h in range(MLA_HEADS)])
    w_vt = jnp.transpose(jnp.take(w_ukv, jnp.asarray(v_src), axis=1)).astype(_bf16)

    return dict(
        w_big=w_big, g_norm=g_norm.reshape(1, d), b_gate=b_gate.reshape(1, -1),
        g_cq=g_cq.reshape(1, -1), w_uq=w_uq_p, g_ckv=g_ckv.reshape(1, -1), w_k=w_k,
        w_vt=w_vt,
        w_pa=w_pa.astype(_bf16), w_pb=w_pb.astype(_bf16), w_out=w_out.astype(_bf16))


def _rope_tables(s, tm):
    pos = np.arange(s, dtype=np.float64)[:, None]
    lane = np.arange(LANES)

    def table(freq_idx, active, n_freq):
        inv = np.power(ROPE_THETA, -2.0 * freq_idx / (2 * n_freq))
        ang = pos * inv[None, :]
        cos = np.where(active[None, :], np.cos(ang), 1.0)
        sign = np.where(lane < 64, -1.0, 1.0)
        sin = np.where(active[None, :], np.sin(ang) * sign[None, :], 0.0)
        return np.stack([cos, sin]).astype(np.float32)

    half = MLA_ROPE // 2
    tab_m = table((lane % 64).clip(0, half - 1).astype(np.float64), (lane % 64) < half, half)
    tab_d = jnp.asarray(table((lane % 32).astype(np.float64), np.ones(LANES, bool), DIL_HEAD_DIM // 2))
    tabs = []
    for _, dil in DIL_GROUPS:
        t = tab_d.reshape(2, s // tm, tm // dil, dil, LANES)
        tabs.append(jnp.swapaxes(t, 2, 3).reshape(2, s, LANES))
    return jnp.asarray(tab_m), tabs


def _trunk(x, mods, layers, g_final):
    b, s, d = x.shape
    assert s % (DIL_GROUPS[-1][1] * BAND_SUB) == 0 and s % TOKEN_TILE == 0
    tab_m, tabs_d = _rope_tables(s, min(TOKEN_TILE, s))
    depth = len(layers)
    for l, lw in enumerate(layers):
        mod = mods[l]
        shift, scale, gate = [mod[:, None, j * d:(j + 1) * d] for j in range(3)]
        q, k, vt, zm, d1, d4, d16, zd, gt = _in_call(x, scale, shift, lw, tab_m, tabs_d)
        ot = _mla_call(q, k, vt)
        outs, lses = [], []
        for qkv, (_, dl) in zip((d1, d4, d16), DIL_GROUPS):
            o_g, lse_g = _band_call(qkv, dl)
            outs.append(o_g)
            lses.append(lse_g)
        x = _out_call(x, ot, zm, outs, lses, zd, gt, gate, lw, g_final.reshape(1, d),
                      final_norm=(l == depth - 1))
    return x


def kernel(x_prompt, x_sample, c_prompt, c_sample, w_ada, b_ada, g_norm, w_in, b_gate, g_cq, w_uq,
           g_ckv, w_ukv, w_pa, w_pb, w_out, g_final):
    depth = w_in.shape[0]
    bp, bs = c_prompt.shape[0], c_sample.shape[0]
    rows = -(-(bp + bs) // 8) * 8
    c_all = jnp.concatenate([c_prompt, c_sample], axis=0)
    c_all = jnp.pad(c_all, ((0, rows - bp - bs), (0, 0)))
    mods = _ada_call(c_all, w_ada, b_ada)
    layers = [_pack_layer(w_in[l], g_norm[l], b_gate[l], g_cq[l], w_uq[l], g_ckv[l], w_ukv[l],
                          w_pa[l], w_pb[l], w_out[l]) for l in range(depth)]
    y_prompt = _trunk(x_prompt, mods[:, :bp], layers, g_final)
    y_sample = _trunk(x_sample, mods[:, bp:bp + bs], layers, g_final)
    return (y_prompt, y_sample)
```

```python
import functools
import math

import numpy as np
import jax
import jax.numpy as jnp
from jax import lax
from jax.experimental import pallas as pl
from jax.experimental.pallas import tpu as pltpu

ROPE_THETA = 10000.0
EPS = 1e-6
NEG_INF = -1e30

MLA_HEADS = 8
MLA_NOPE = 64
MLA_ROPE = 32
MLA_V = 64
Q_LORA = 384
KV_LORA = 256
MLA_WIDTH = MLA_HEADS * MLA_V
DIL_GROUPS = ((128, 1), (512, 4), (2048, 16))
DIL_HEADS = 8
DIL_HEAD_DIM = 64
DIL_WIDTH = DIL_HEADS * DIL_HEAD_DIM
N_GROUPS = len(DIL_GROUPS)
BAND_HALF = 64

LANES = 128
KR_PAD = LANES
HEAD_PAD = LANES
DIL_SLABS = DIL_WIDTH // LANES
LSE_LANES = LANES // DIL_HEADS

OFF_CQ = 0
OFF_CKV = OFF_CQ + Q_LORA
OFF_KR = OFF_CKV + KV_LORA
OFF_ZM = OFF_KR + KR_PAD
OFF_DIL = OFF_ZM + MLA_WIDTH
OFF_ZD = OFF_DIL + 3 * N_GROUPS * DIL_WIDTH
OFF_MG = OFF_ZD + DIL_WIDTH
W_BIG = OFF_MG + 2 * 1024

TOKEN_TILE = 512
MLA_Q_TILE = 1024
MLA_K_CHUNK = 512
MLA_UNROLL = 8
BAND_SUB = 128
BAND_ROWS = 1024
VMEM_LIMIT = 56 * 1024 * 1024

_f32 = jnp.float32
_bf16 = jnp.bfloat16
_NT = (((1,), (1,)), ((), ()))


def _dot(a, b):
    return jnp.dot(a, b, preferred_element_type=_f32)


def _dot_nt(a, b):
    return lax.dot_general(a, b, _NT, preferred_element_type=_f32)


def _rms(x, g):
    return x * lax.rsqrt(jnp.mean(x * x, axis=-1, keepdims=True) + EPS) * g


def _sigmoid(x):
    return 1.0 / (1.0 + jnp.exp(-x))


def _rope_pairs(x, cos, sin):
    return x * cos + pltpu.roll(x, 64, axis=1) * sin


def _ada_kernel(c_ref, w_ref, b_ref, o_ref):
    c = c_ref[...]
    a = c * _sigmoid(c)
    a_hi = a.astype(_bf16)
    a_lo = (a - a_hi.astype(_f32)).astype(_bf16)
    w = w_ref[0]
    w_hi = w.astype(_bf16)
    w_lo = (w - w_hi.astype(_f32)).astype(_bf16)
    o_ref[0] = _dot(a_hi, w_hi) + _dot(a_hi, w_lo) + _dot(a_lo, w_hi) + b_ref[0]


def _ada_call(c_all, w_ada, b_ada):
    depth, d, n = w_ada.shape
    rows = c_all.shape[0]
    nb = n // d
    return pl.pallas_call(
        _ada_kernel,
        grid=(depth, nb),
        in_specs=[
            pl.BlockSpec((rows, d), lambda l, j: (0, 0)),
            pl.BlockSpec((1, d, d), lambda l, j: (l, 0, j)),
            pl.BlockSpec((1, 1, d), lambda l, j: (l, 0, j)),
        ],
        out_specs=pl.BlockSpec((1, rows, d), lambda l, j: (l, 0, j)),
        out_shape=jax.ShapeDtypeStruct((depth, rows, n), _f32),
        compiler_params=pltpu.CompilerParams(vmem_limit_bytes=VMEM_LIMIT),
        name="adaln",
    )(c_all, w_ada, b_ada.reshape(depth, 1, n))


def _in_kernel(x_ref, sc_ref, sh_ref, gn_ref, w_ref, gcq_ref, wuq_ref, gckv_ref, wk_ref,
               wvt_ref, bg_ref, tm_ref, t1_ref, t4_ref, t16_ref,
               q_ref, k_ref, vt_ref, zm_ref, d1_ref, d4_ref, d16_ref, zd_ref, gt_ref,
               hs_ref, *, q_scale):
    x = x_ref[0]
    rows, d_model = x.shape
    h = _rms(x, gn_ref[...]) * (1.0 + sc_ref[0]) + sh_ref[0]
    hb = h.astype(_bf16)
    for c in range(d_model // LANES):
        hs_ref[c] = h[:, c * LANES:(c + 1) * LANES]
    cos_m, sin_m = tm_ref[0], tm_ref[1]

    cq = _dot(hb, w_ref[:, OFF_CQ:OFF_CQ + Q_LORA])
    cqn = _rms(cq, gcq_ref[...]).astype(_bf16)
    q = _dot(cqn, wuq_ref[...])
    for hd in range(MLA_HEADS):
        sl = slice(hd * HEAD_PAD, (hd + 1) * HEAD_PAD)
        q_ref[0, hd] = (_rope_pairs(q[:, sl], cos_m, sin_m) * q_scale).astype(_bf16)

    ckv = _dot(hb, w_ref[:, OFF_CKV:OFF_CKV + KV_LORA])
    ckvn = _rms(ckv, gckv_ref[...]).astype(_bf16)
    kr = _dot(hb, w_ref[:, OFF_KR:OFF_KR + KR_PAD])
    k = _dot(ckvn, wk_ref[...])
    for hd in range(MLA_HEADS):
        sl = slice(hd * HEAD_PAD, (hd + 1) * HEAD_PAD)
        k_ref[0, hd] = _rope_pairs(k[:, sl] + kr, cos_m, sin_m).astype(_bf16)
    vt = _dot_nt(wvt_ref[...], ckvn)
    for hd in range(MLA_HEADS):
        vt_ref[0, hd, 0] = vt[hd * MLA_V:(hd + 1) * MLA_V, :].astype(_bf16)

    zm = _dot(hb, w_ref[:, OFF_ZM:OFF_ZM + MLA_WIDTH])
    zm_ref[0] = (zm * _sigmoid(zm)).astype(_bf16)

    zd = _dot(hb, w_ref[:, OFF_ZD:OFF_ZD + DIL_WIDTH])
    zd_ref[0] = (zd * _sigmoid(zd)).astype(_bf16)

    mg = _dot(hb, w_ref[:, OFF_MG:W_BIG]) + bg_ref[...]
    gt_ref[0] = _sigmoid(mg).astype(_bf16)

    for g, (dil, out_ref, tab_ref) in enumerate(((1, d1_ref, t1_ref), (4, d4_ref, t4_ref),
                                                 (16, d16_ref, t16_ref))):
        per = rows // dil
        if dil == 1:
            hg = hb
        else:
            hg = jnp.concatenate(
                [jnp.concatenate([hs_ref[c, pl.ds(r, per, stride=dil), :] for r in range(dil)], axis=0)
                 for c in range(d_model // LANES)], axis=1).astype(_bf16)
        cos_d, sin_d = tab_ref[0], tab_ref[1]
        for kind in range(3):
            off = OFF_DIL + (3 * g + kind) * DIL_WIDTH
            u = _dot(hg, w_ref[:, off:off + DIL_WIDTH])
            if kind == 2:
                ub = u.astype(_bf16)
            else:
                post = DIL_HEAD_DIM ** -0.5 * math.log2(math.e) if kind == 0 else 1.0
                ub = jnp.concatenate(
                    [(_rope_pairs(u[:, p * LANES:(p + 1) * LANES], cos_d, sin_d) * post).astype(_bf16)
                     for p in range(DIL_SLABS)], axis=1)
            for r in range(dil):
                out_ref[kind, 0, r] = ub[r * per:(r + 1) * per]


def _in_call(x, scale, shift, lw, tab_m, tabs_d):
    b, s, d = x.shape
    tm = min(TOKEN_TILE, s)
    nt = s // tm
    const = lambda *shape: pl.BlockSpec(shape, lambda bi, ti: (0,) * len(shape),
                                        pipeline_mode=pl.Buffered(1))
    tab = pl.BlockSpec((2, tm, LANES), lambda bi, ti: (0, ti, 0))
    in_specs = [
        pl.BlockSpec((1, tm, d), lambda bi, ti: (bi, ti, 0)),
        pl.BlockSpec((1, 1, d), lambda bi, ti: (bi, 0, 0)),
        pl.BlockSpec((1, 1, d), lambda bi, ti: (bi, 0, 0)),
        const(1, d),
        const(d, W_BIG),
        const(1, Q_LORA),
        const(Q_LORA, MLA_HEADS * HEAD_PAD),
        const(1, KV_LORA),
        const(KV_LORA, MLA_HEADS * HEAD_PAD),
        const(MLA_WIDTH, KV_LORA),
        const(1, 2 * d),
        tab, tab, tab, tab,
    ]
    dil_shapes = [jax.ShapeDtypeStruct((3, b, dl, s // dl, DIL_WIDTH), _bf16) for _, dl in DIL_GROUPS]
    dil_specs = [pl.BlockSpec((3, 1, dl, tm // dl, DIL_WIDTH), lambda bi, ti: (0, bi, 0, ti, 0))
                 for _, dl in DIL_GROUPS]
    out_shape = [
        jax.ShapeDtypeStruct((b, MLA_HEADS, s, HEAD_PAD), _bf16),
        jax.ShapeDtypeStruct((b, MLA_HEADS, s, HEAD_PAD), _bf16),
        jax.ShapeDtypeStruct((b, MLA_HEADS, nt, MLA_V, tm), _bf16),
        jax.ShapeDtypeStruct((b, s, MLA_WIDTH), _bf16),
        *dil_shapes,
        jax.ShapeDtypeStruct((b, s, DIL_WIDTH), _bf16),
        jax.ShapeDtypeStruct((b, s, 2 * d), _bf16),
    ]
    out_specs = [
        pl.BlockSpec((1, MLA_HEADS, tm, HEAD_PAD), lambda bi, ti: (bi, 0, ti, 0)),
        pl.BlockSpec((1, MLA_HEADS, tm, HEAD_PAD), lambda bi, ti: (bi, 0, ti, 0)),
        pl.BlockSpec((1, MLA_HEADS, 1, MLA_V, tm), lambda bi, ti: (bi, 0, ti, 0, 0)),
        pl.BlockSpec((1, tm, MLA_WIDTH), lambda bi, ti: (bi, ti, 0)),
        *dil_specs,
        pl.BlockSpec((1, tm, DIL_WIDTH), lambda bi, ti: (bi, ti, 0)),
        pl.BlockSpec((1, tm, 2 * d), lambda bi, ti: (bi, ti, 0)),
    ]
    kern = functools.partial(_in_kernel, q_scale=(MLA_NOPE + MLA_ROPE) ** -0.5 * math.log2(math.e))
    return pl.pallas_call(
        kern,
        grid=(b, nt),
        in_specs=in_specs,
        out_specs=out_specs,
        out_shape=out_shape,
        scratch_shapes=[pltpu.VMEM((d // LANES, tm, LANES), _f32)],
        compiler_params=pltpu.CompilerParams(vmem_limit_bytes=VMEM_LIMIT),
        name="in_proj",
    )(x, scale, shift, lw["g_norm"], lw["w_big"], lw["g_cq"], lw["w_uq"], lw["g_ckv"],
      lw["w_k"], lw["w_vt"], lw["b_gate"], tab_m, *tabs_d)


def _mla_kernel(q_ref, k_ref, vt_ref, o_ref, sa_ref, sb_ref, m_ref, acc_ref, *, n_tiles, tq,
                n_chunks, tk, unroll):
    ones = jnp.ones((16, tk), _bf16)
    per = tk // vt_ref.shape[-1]
    bufs = (sa_ref, sb_ref)

    def scores(qi, c, dst_ref):
        q = q_ref[0, 0, pl.ds(pl.multiple_of(qi * tq, tq), tq), :]
        kc = k_ref[0, 0, pl.ds(pl.multiple_of(c * tk, tk), tk), :]
        st = _dot_nt(kc, q)
        dst_ref[...] = st
        return jnp.max(st, axis=0, keepdims=True)

    def consume(c, src_ref, cmax):
        m_old = m_ref[...]
        m_new = jnp.maximum(m_old, cmax)
        alpha = jnp.exp2(m_old - m_new)
        p = jnp.exp2(src_ref[...] - m_new).astype(_bf16)
        vc = jnp.concatenate([vt_ref[0, 0, c * per + j] for j in range(per)], axis=1)
        va = jnp.concatenate([vc, ones], axis=0)
        acc_ref[...] = alpha * acc_ref[...] + _dot(va, p)
        m_ref[...] = m_new

    def steps(qi, first, count, cm):
        for u in range(count):
            cm_next = scores(qi, first + u + 1, bufs[(u + 1) % 2])
            consume(first + u, bufs[u % 2], cm)
            cm = cm_next
        return cm

    def tile(qi, cm):
        m_ref[...] = jnp.full(m_ref.shape, NEG_INF, _f32)
        acc_ref[...] = jnp.zeros(acc_ref.shape, _f32)
        cm = lax.fori_loop(0, n_chunks // unroll - 1,
                           lambda j, cm: steps(qi, j * unroll, unroll, cm), cm)
        cm = steps(qi, n_chunks - unroll, unroll - 1, cm)
        cm_next = scores(jnp.minimum(qi + 1, n_tiles - 1), 0, bufs[0])
        consume(n_chunks - 1, bufs[1], cm)
        acc = acc_ref[...]
        o_ref[0, 0, qi] = (acc[:MLA_V] / acc[MLA_V:MLA_V + 1]).astype(_bf16)
        return cm_next

    lax.fori_loop(0, n_tiles, tile, scores(0, 0, sa_ref))


def _mla_call(q, k, vt):
    b, hh, s, _ = q.shape
    nv, tv = vt.shape[2], vt.shape[4]
    tk = min(MLA_K_CHUNK, s // 2)
    n_chunks = s // tk
    unroll = MLA_UNROLL if n_chunks >= 4 * MLA_UNROLL else 2
    assert n_chunks % unroll == 0 and unroll % 2 == 0 and tk % tv == 0
    tq = min(MLA_Q_TILE, s)
    n_tiles = s // tq
    kern = functools.partial(_mla_kernel, n_tiles=n_tiles, tq=tq, n_chunks=n_chunks, tk=tk,
                             unroll=unroll)
    return pl.pallas_call(
        kern,
        grid=(b, hh),
        in_specs=[
            pl.BlockSpec((1, 1, s, HEAD_PAD), lambda bi, hi: (bi, hi, 0, 0)),
            pl.BlockSpec((1, 1, s, HEAD_PAD), lambda bi, hi: (bi, hi, 0, 0)),
            pl.BlockSpec((1, 1, nv, MLA_V, tv), lambda bi, hi: (bi, hi, 0, 0, 0)),
        ],
        out_specs=pl.BlockSpec((1, 1, n_tiles, MLA_V, tq), lambda bi, hi: (bi, hi, 0, 0, 0)),
        out_shape=jax.ShapeDtypeStruct((b, hh, n_tiles, MLA_V, tq), _bf16),
        scratch_shapes=[pltpu.VMEM((tk, tq), _f32), pltpu.VMEM((tk, tq), _f32),
                        pltpu.VMEM((1, tq), _f32), pltpu.VMEM((MLA_V + 16, tq), _f32)],
        compiler_params=pltpu.CompilerParams(vmem_limit_bytes=VMEM_LIMIT),
        name="mla_attn",
    )(q, k, vt)


def _band_kernel(q_ref, kp_ref, kc_ref, kn_ref, vp_ref, vc_ref, vn_ref, o_ref, stat_ref,
                 *, dil, tq, seq):
    i = pl.program_id(1)
    nk = BAND_SUB + 2 * BAND_HALF
    row = lax.broadcasted_iota(jnp.int32, (BAND_SUB, nk), 0)
    col = lax.broadcasted_iota(jnp.int32, (BAND_SUB, nk), 1)
    band = jnp.abs(col - BAND_HALF - row) <= BAND_HALF
    lane = lax.broadcasted_iota(jnp.int32, (1, LANES), 1)
    first_head = (lane % 64) < 32
    low_half = lane < 64

    def residue(r, carry):
        q = q_ref[0, 0, r]
        kk = jnp.concatenate([kp_ref[0, 0, r], kc_ref[0, 0, r], kn_ref[0, 0, r]], axis=0)
        vv = jnp.concatenate([vp_ref[0, 0, r], vc_ref[0, 0, r], vn_ref[0, 0, r]], axis=0)
        for sb in range(tq // BAND_SUB):
            a = sb * BAND_SUB
            if 0 < sb < tq // BAND_SUB - 1:
                valid = band
            else:
                kidx = i * tq + (a - BAND_HALF) + col
                valid = band & (kidx >= 0) & (kidx < seq)
            dst = pl.ds(r + a * dil, BAND_SUB, stride=dil) if dil > 1 else pl.ds(a, BAND_SUB)
            stats = [None, None]
            for p in range(DIL_SLABS):
                sl = slice(p * LANES, (p + 1) * LANES)
                qp = q[a:a + BAND_SUB, sl]
                kp = kk[a:a + nk, sl]
                vp = vv[a:a + nk, sl]
                res = []
                for sel in (first_head, jnp.logical_not(first_head)):
                    qm = jnp.where(sel, qp, jnp.zeros_like(qp))
                    sc = jnp.where(valid, _dot_nt(qm, kp), NEG_INF)
                    m = jnp.max(sc, axis=-1, keepdims=True)
                    e = jnp.exp2(sc - m)
                    l = jnp.sum(e, axis=-1, keepdims=True)
                    res.append((_dot(e.astype(_bf16), vp), m, l))
                o_ref[0, p, dst, :] = jnp.where(low_half, res[0][0], res[1][0])
                even = lane % 32 < LSE_LANES
                here = lane // 32 == p
                for j in range(2):
                    pair = jnp.where(even, res[0][1 + j], res[1][1 + j])
                    stats[j] = pair if p == 0 else jnp.where(here, pair, stats[j])
            stat_ref[0, 0, dst, :] = stats[0]
            stat_ref[0, 1, dst, :] = stats[1]
        return carry

    lax.fori_loop(0, dil, residue, 0, unroll=min(dil, 8))


def _band_call(qkv, dil):
    _, b, _, seq, w = qkv.shape
    s = seq * dil
    tq = max(BAND_SUB, BAND_ROWS // dil)
    assert tq % BAND_SUB == 0 and seq % tq == 0
    hb = tq // BAND_HALF
    last = seq // BAND_HALF - 1

    def cur(j):
        return pl.BlockSpec((1, 1, dil, tq, w), lambda bi, i: (j, bi, 0, i, 0))

    def prev(j):
        return pl.BlockSpec((1, 1, dil, BAND_HALF, w),
                            lambda bi, i: (j, bi, 0, jnp.maximum(i * hb - 1, 0), 0))

    def nxt(j):
        return pl.BlockSpec((1, 1, dil, BAND_HALF, w),
                            lambda bi, i: (j, bi, 0, jnp.minimum((i + 1) * hb, last), 0))

    kern = functools.partial(_band_kernel, dil=dil, tq=tq, seq=seq)
    return pl.pallas_call(
        kern,
        grid=(b, seq // tq),
        in_specs=[cur(0), prev(1), cur(1), nxt(1), prev(2), cur(2), nxt(2)],
        out_specs=[pl.BlockSpec((1, DIL_SLABS, dil * tq, LANES), lambda bi, i: (bi, 0, i, 0)),
                   pl.BlockSpec((1, 2, dil * tq, LANES), lambda bi, i: (bi, 0, i, 0))],
        out_shape=[jax.ShapeDtypeStruct((b, DIL_SLABS, s, LANES), _f32),
                   jax.ShapeDtypeStruct((b, 2, s, LANES), _f32)],
        compiler_params=pltpu.CompilerParams(vmem_limit_bytes=VMEM_LIMIT),
        name=f"band_attn_d{dil}",
    )(qkv, qkv, qkv, qkv, qkv, qkv, qkv)


def _out_kernel(x_ref, ot_ref, zm_ref, o0_ref, o1_ref, o2_ref, l0_ref, l1_ref, l2_ref,
                zd_ref, gt_ref, gate_ref, wpa_ref, wpb_ref, wo_ref, gf_ref, ex_ref, y_ref,
                *, final_norm):
    d = x_ref.shape[-1]
    ot = jnp.concatenate([ot_ref[0, hd, 0] for hd in range(MLA_HEADS)], axis=0)
    o_mla = jnp.transpose(ot.astype(_f32)) * zm_ref[0].astype(_f32)
    a = _dot(o_mla.astype(_bf16), wpa_ref[...])

    m0, m1, m2 = l0_ref[0, 0], l1_ref[0, 0], l2_ref[0, 0]
    mx = jnp.maximum(jnp.maximum(m0, m1), m2)
    w0, w1, w2 = jnp.exp2(m0 - mx), jnp.exp2(m1 - mx), jnp.exp2(m2 - mx)
    inv = 1.0 / (w0 * l0_ref[0, 1] + w1 * l1_ref[0, 1] + w2 * l2_ref[0, 1])
    o_dil = None
    for w, o_ref in ((w0, o0_ref), (w1, o1_ref), (w2, o2_ref)):
        wide = _dot((w * inv).astype(_bf16), ex_ref[...])
        term = wide * jnp.concatenate([o_ref[0, p] for p in range(DIL_SLABS)], axis=1)
        o_dil = term if o_dil is None else o_dil + term
    o_dil = o_dil * zd_ref[0].astype(_f32)
    bb = _dot(o_dil.astype(_bf16), wpb_ref[...])

    gt = gt_ref[0].astype(_f32)
    u = gt[:, :d] * a + gt[:, d:] * bb
    y = x_ref[0] + gate_ref[0] * _dot(u.astype(_bf16), wo_ref[...])
    if final_norm:
        y = _rms(y, gf_ref[...])
    y_ref[0] = y


def _out_call(x, ot, zm, outs, lses, zd, gt, gate, lw, g_final, final_norm):
    b, s, d = x.shape
    tm = min(TOKEN_TILE, s)
    tok = lambda w: pl.BlockSpec((1, tm, w), lambda bi, ti: (bi, ti, 0))
    slab = pl.BlockSpec((1, DIL_SLABS, tm, LANES), lambda bi, ti: (bi, 0, ti, 0))
    stat = pl.BlockSpec((1, 2, tm, LANES), lambda bi, ti: (bi, 0, ti, 0))
    const = lambda *shape: pl.BlockSpec(shape, lambda bi, ti: (0,) * len(shape))
    kern = functools.partial(_out_kernel, final_norm=final_norm)
    assert ot.shape[-1] % tm == 0
    per = ot.shape[-1] // tm
    spread = np.zeros((LANES, DIL_WIDTH), np.float32)
    for hd in range(DIL_HEADS):
        spread[hd * LSE_LANES, hd * DIL_HEAD_DIM:(hd + 1) * DIL_HEAD_DIM] = 1.0
    return pl.pallas_call(
        kern,
        grid=(b, s // tm),
        in_specs=[tok(d),
                  pl.BlockSpec((1, MLA_HEADS, 1, MLA_V, tm),
                               lambda bi, ti: (bi, 0, ti // per, 0, ti % per)),
                  tok(MLA_WIDTH),
                  slab, slab, slab, stat, stat, stat,
                  tok(DIL_WIDTH), tok(2 * d),
                  pl.BlockSpec((1, 1, d), lambda bi, ti: (bi, 0, 0)),
                  const(MLA_WIDTH, d), const(DIL_WIDTH, d), const(d, d), const(1, d),
                  const(LANES, DIL_WIDTH)],
        out_specs=tok(d),
        out_shape=jax.ShapeDtypeStruct((b, s, d), _f32),
        compiler_params=pltpu.CompilerParams(vmem_limit_bytes=VMEM_LIMIT),
        name="out_proj",
    )(x, ot, zm, outs[0], outs[1], outs[2], lses[0], lses[1], lses[2],
      zd, gt, gate, lw["w_pa"], lw["w_pb"], lw["w_out"], g_final, jnp.asarray(spread, _bf16))


def _mla_lane_order():
    half = MLA_ROPE // 2
    src = np.full((HEAD_PAD,), -1, np.int64)
    src[0:half] = MLA_NOPE + np.arange(half)
    src[half:64] = np.arange(64 - half)
    src[64:64 + half] = MLA_NOPE + half + np.arange(half)
    src[64 + half:64 + half + (MLA_NOPE - (64 - half))] = np.arange(64 - half, MLA_NOPE)
    return src


def _gather_cols(w, src):
    picked = jnp.take(w, jnp.asarray(np.maximum(src, 0)), axis=1)
    return jnp.where(jnp.asarray(src >= 0)[None, :], picked, 0.0)


def _dil_perm():
    idx = []
    half = DIL_HEAD_DIM // 2
    for p in range(DIL_HEADS // 2):
        ha, hb = 2 * p * DIL_HEAD_DIM, (2 * p + 1) * DIL_HEAD_DIM
        idx += list(range(ha, ha + half)) + list(range(hb, hb + half))
        idx += list(range(ha + half, ha + 2 * half)) + list(range(hb + half, hb + 2 * half))
    return np.asarray(idx, np.int64)


def _pack_layer(w_in, g_norm, b_gate, g_cq, w_uq, g_ckv, w_ukv, w_pa, w_pb, w_out):
    d = w_in.shape[0]
    splits = np.cumsum([Q_LORA, KV_LORA, MLA_ROPE, MLA_WIDTH] + [DIL_WIDTH] * (3 * N_GROUPS)
                       + [DIL_WIDTH])
    parts = jnp.split(w_in, list(splits), axis=1)
    perm = _dil_perm()
    lane_src = _mla_lane_order()
    kr_src = np.where(lane_src >= MLA_NOPE, lane_src - MLA_NOPE, -1)
    cols = [parts[0], parts[1], _gather_cols(parts[2], kr_src), parts[3]]
    for j in range(3 * N_GROUPS):
        pj = parts[4 + j]
        cols.append(pj if j % 3 == 2 else jnp.take(pj, jnp.asarray(perm), axis=1))
    cols += [parts[4 + 3 * N_GROUPS], parts[5 + 3 * N_GROUPS]]
    w_big = jnp.concatenate(cols, axis=1).astype(_bf16)

    dq = MLA_NOPE + MLA_ROPE
    q_src = np.concatenate([np.where(lane_src >= 0, h * dq + lane_src, -1) for h in range(MLA_HEADS)])
    w_uq_p = _gather_cols(w_uq, q_src).astype(_bf16)
    dkv = MLA_NOPE + MLA_V
    nope_src = np.where((lane_src >= 0) & (lane_src < MLA_NOPE), lane_src, -1)
    k_src = np.concatenate([np.where(nope_src >= 0, h * dkv + nope_src, -1) for h in range(MLA_HEADS)])
    w_k = _gather_cols(w_ukv, k_src).astype(_bf16)
    v_src = np.concatenate([h * dkv + MLA_NOPE + np.arange(MLA_V) for h in range(MLA_HEADS)])
    w_vt = jnp.transpose(jnp.take(w_ukv, jnp.asarray(v_src), axis=1)).astype(_bf16)

    return dict(
        w_big=w_big, g_norm=g_norm.reshape(1, d), b_gate=b_gate.reshape(1, -1),
        g_cq=g_cq.reshape(1, -1), w_uq=w_uq_p, g_ckv=g_ckv.reshape(1, -1), w_k=w_k,
        w_vt=w_vt,
        w_pa=w_pa.astype(_bf16), w_pb=w_pb.astype(_bf16), w_out=w_out.astype(_bf16))


def _rope_tables(s, tm):
    pos = np.arange(s, dtype=np.float64)[:, None]
    lane = np.arange(LANES)

    def table(freq_idx, active, n_freq):
        inv = np.power(ROPE_THETA, -2.0 * freq_idx / (2 * n_freq))
        ang = pos * inv[None, :]
        cos = np.where(active[None, :], np.cos(ang), 1.0)
        sign = np.where(lane < 64, -1.0, 1.0)
        sin = np.where(active[None, :], np.sin(ang) * sign[None, :], 0.0)
        return np.stack([cos, sin]).astype(np.float32)

    half = MLA_ROPE // 2
    tab_m = table((lane % 64).clip(0, half - 1).astype(np.float64), (lane % 64) < half, half)
    tab_d = jnp.asarray(table((lane % 32).astype(np.float64), np.ones(LANES, bool), DIL_HEAD_DIM // 2))
    tabs = []
    for _, dil in DIL_GROUPS:
        t = tab_d.reshape(2, s // tm, tm // dil, dil, LANES)
        tabs.append(jnp.swapaxes(t, 2, 3).reshape(2, s, LANES))
    return jnp.asarray(tab_m), tabs


def _trunk(x, mods, layers, g_final):
    b, s, d = x.shape
    assert s % (DIL_GROUPS[-1][1] * BAND_SUB) == 0 and s % TOKEN_TILE == 0
    tab_m, tabs_d = _rope_tables(s, min(TOKEN_TILE, s))
    depth = len(layers)
    for l, lw in enumerate(layers):
        mod = mods[l]
        shift, scale, gate = [mod[:, None, j * d:(j + 1) * d] for j in range(3)]
        q, k, vt, zm, d1, d4, d16, zd, gt = _in_call(x, scale, shift, lw, tab_m, tabs_d)
        ot = _mla_call(q, k, vt)
        outs, lses = [], []
        for qkv, (_, dl) in zip((d1, d4, d16), DIL_GROUPS):
            o_g, lse_g = _band_call(qkv, dl)
            outs.append(o_g)
            lses.append(lse_g)
        x = _out_call(x, ot, zm, outs, lses, zd, gt, gate, lw, g_final.reshape(1, d),
                      final_norm=(l == depth - 1))
    return x


def kernel(x_prompt, x_sample, c_prompt, c_sample, w_ada, b_ada, g_norm, w_in, b_gate, g_cq, w_uq,
           g_ckv, w_ukv, w_pa, w_pb, w_out, g_final):
    depth = w_in.shape[0]
    bp, bs = c_prompt.shape[0], c_sample.shape[0]
    rows = -(-(bp + bs) // 8) * 8
    c_all = jnp.concatenate([c_prompt, c_sample], axis=0)
    c_all = jnp.pad(c_all, ((0, rows - bp - bs), (0, 0)))
    mods = _ada_call(c_all, w_ada, b_ada)
    layers = [_pack_layer(w_in[l], g_norm[l], b_gate[l], g_cq[l], w_uq[l], g_ckv[l], w_ukv[l],
                          w_pa[l], w_pb[l], w_out[l]) for l in range(depth)]
    y_prompt = _trunk(x_prompt, mods[:, :bp], layers, g_final)
    y_sample = _trunk(x_sample, mods[:, bp:bp + bs], layers, g_final)
    return (y_prompt, y_sample)
```

```python
import functools
import math

import numpy as np
import jax
import jax.numpy as jnp
from jax import lax
from jax.experimental import pallas as pl
from jax.experimental.pallas import tpu as pltpu

ROPE_THETA = 10000.0
EPS = 1e-6
NEG_INF = -1e30

MLA_HEADS = 8
MLA_NOPE = 64
MLA_ROPE = 32
MLA_V = 64
Q_LORA = 384
KV_LORA = 256
MLA_WIDTH = MLA_HEADS * MLA_V
DIL_GROUPS = ((128, 1), (512, 4), (2048, 16))
DIL_HEADS = 8
DIL_HEAD_DIM = 64
DIL_WIDTH = DIL_HEADS * DIL_HEAD_DIM
N_GROUPS = len(DIL_GROUPS)
BAND_HALF = 64

LANES = 128
KR_PAD = LANES
HEAD_PAD = LANES
DIL_SLABS = DIL_WIDTH // LANES
LSE_LANES = LANES // DIL_HEADS

OFF_CQ = 0
OFF_CKV = OFF_CQ + Q_LORA
OFF_KR = OFF_CKV + KV_LORA
OFF_ZM = OFF_KR + KR_PAD
OFF_DIL = OFF_ZM + MLA_WIDTH
OFF_ZD = OFF_DIL + 3 * N_GROUPS * DIL_WIDTH
OFF_MG = OFF_ZD + DIL_WIDTH
W_BIG = OFF_MG + 2 * 1024

TOKEN_TILE = 512
MLA_Q_TILE = 512
MLA_K_CHUNK = 1024
MLA_UNROLL = 4
BAND_SUB = 128
BAND_ROWS = 2048
VMEM_LIMIT = 56 * 1024 * 1024

_f32 = jnp.float32
_bf16 = jnp.bfloat16
_NT = (((1,), (1,)), ((), ()))


def _dot(a, b):
    return jnp.dot(a, b, preferred_element_type=_f32)


def _dot_nt(a, b):
    return lax.dot_general(a, b, _NT, preferred_element_type=_f32)


def _rms(x, g):
    return x * lax.rsqrt(jnp.mean(x * x, axis=-1, keepdims=True) + EPS) * g


def _sigmoid(x):
    return 1.0 / (1.0 + jnp.exp(-x))


def _rope_pairs(x, cos, sin):
    return x * cos + pltpu.roll(x, 64, axis=1) * sin


def _ada_kernel(c_ref, w_ref, b_ref, o_ref):
    c = c_ref[...]
    a = c * _sigmoid(c)
    a_hi = a.astype(_bf16)
    a_lo = (a - a_hi.astype(_f32)).astype(_bf16)
    w = w_ref[0]
    w_hi = w.astype(_bf16)
    w_lo = (w - w_hi.astype(_f32)).astype(_bf16)
    o_ref[0] = _dot(a_hi, w_hi) + _dot(a_hi, w_lo) + _dot(a_lo, w_hi) + b_ref[0]


def _ada_call(c_all, w_ada, b_ada):
    depth, d, n = w_ada.shape
    rows = c_all.shape[0]
    nb = n // d
    return pl.pallas_call(
        _ada_kernel,
        grid=(depth, nb),
        in_specs=[
            pl.BlockSpec((rows, d), lambda l, j: (0, 0)),
            pl.BlockSpec((1, d, d), lambda l, j: (l, 0, j)),
            pl.BlockSpec((1, 1, d), lambda l, j: (l, 0, j)),
        ],
        out_specs=pl.BlockSpec((1, rows, d), lambda l, j: (l, 0, j)),
        out_shape=jax.ShapeDtypeStruct((depth, rows, n), _f32),
        compiler_params=pltpu.CompilerParams(vmem_limit_bytes=VMEM_LIMIT),
        name="adaln",
    )(c_all, w_ada, b_ada.reshape(depth, 1, n))


def _in_kernel(x_ref, sc_ref, sh_ref, gn_ref, w_ref, gcq_ref, wuq_ref, gckv_ref, wk_ref,
               wvt_ref, bg_ref, tm_ref, t1_ref, t4_ref, t16_ref,
               q_ref, k_ref, vt_ref, zm_ref, d1_ref, d4_ref, d16_ref, zd_ref, gt_ref,
               hs_ref, *, q_scale):
    x = x_ref[0]
    rows, d_model = x.shape
    h = _rms(x, gn_ref[...]) * (1.0 + sc_ref[0]) + sh_ref[0]
    hb = h.astype(_bf16)
    for c in range(d_model // LANES):
        hs_ref[c] = h[:, c * LANES:(c + 1) * LANES]
    cos_m, sin_m = tm_ref[0], tm_ref[1]

    cq = _dot(hb, w_ref[:, OFF_CQ:OFF_CQ + Q_LORA])
    cqn = _rms(cq, gcq_ref[...]).astype(_bf16)
    q = _dot(cqn, wuq_ref[...])
    for hd in range(MLA_HEADS):
        sl = slice(hd * HEAD_PAD, (hd + 1) * HEAD_PAD)
        q_ref[0, hd] = (_rope_pairs(q[:, sl], cos_m, sin_m) * q_scale).astype(_bf16)

    ckv = _dot(hb, w_ref[:, OFF_CKV:OFF_CKV + KV_LORA])
    ckvn = _rms(ckv, gckv_ref[...]).astype(_bf16)
    kr = _dot(hb, w_ref[:, OFF_KR:OFF_KR + KR_PAD])
    k = _dot(ckvn, wk_ref[...])
    for hd in range(MLA_HEADS):
        sl = slice(hd * HEAD_PAD, (hd + 1) * HEAD_PAD)
        k_ref[0, hd] = _rope_pairs(k[:, sl] + kr, cos_m, sin_m).astype(_bf16)
    vt = _dot_nt(wvt_ref[...], ckvn)
    for hd in range(MLA_HEADS):
        vt_ref[0, hd, 0] = vt[hd * MLA_V:(hd + 1) * MLA_V, :].astype(_bf16)

    zm = _dot(hb, w_ref[:, OFF_ZM:OFF_ZM + MLA_WIDTH])
    zm_ref[0] = (zm * _sigmoid(zm)).astype(_bf16)

    zd = _dot(hb, w_ref[:, OFF_ZD:OFF_ZD + DIL_WIDTH])
    zd_ref[0] = (zd * _sigmoid(zd)).astype(_bf16)

    mg = _dot(hb, w_ref[:, OFF_MG:W_BIG]) + bg_ref[...]
    gt_ref[0] = _sigmoid(mg).astype(_bf16)

    for g, (dil, out_ref, tab_ref) in enumerate(((1, d1_ref, t1_ref), (4, d4_ref, t4_ref),
                                                 (16, d16_ref, t16_ref))):
        per = rows // dil
        if dil == 1:
            hg = hb
        else:
            hg = jnp.concatenate(
                [jnp.concatenate([hs_ref[c, pl.ds(r, per, stride=dil), :] for r in range(dil)], axis=0)
                 for c in range(d_model // LANES)], axis=1).astype(_bf16)
        cos_d, sin_d = tab_ref[0], tab_ref[1]
        for kind in range(3):
            off = OFF_DIL + (3 * g + kind) * DIL_WIDTH
            u = _dot(hg, w_ref[:, off:off + DIL_WIDTH])
            if kind == 2:
                ub = u.astype(_bf16)
            else:
                post = DIL_HEAD_DIM ** -0.5 * math.log2(math.e) if kind == 0 else 1.0
                ub = jnp.concatenate(
                    [(_rope_pairs(u[:, p * LANES:(p + 1) * LANES], cos_d, sin_d) * post).astype(_bf16)
                     for p in range(DIL_SLABS)], axis=1)
            for r in range(dil):
                out_ref[kind, 0, r] = ub[r * per:(r + 1) * per]


def _in_call(x, scale, shift, lw, tab_m, tabs_d):
    b, s, d = x.shape
    tm = min(TOKEN_TILE, s)
    nt = s // tm
    const = lambda *shape: pl.BlockSpec(shape, lambda bi, ti: (0,) * len(shape),
                                        pipeline_mode=pl.Buffered(1))
    tab = pl.BlockSpec((2, tm, LANES), lambda bi, ti: (0, ti, 0))
    in_specs = [
        pl.BlockSpec((1, tm, d), lambda bi, ti: (bi, ti, 0)),
        pl.BlockSpec((1, 1, d), lambda bi, ti: (bi, 0, 0)),
        pl.BlockSpec((1, 1, d), lambda bi, ti: (bi, 0, 0)),
        const(1, d),
        const(d, W_BIG),
        const(1, Q_LORA),
        const(Q_LORA, MLA_HEADS * HEAD_PAD),
        const(1, KV_LORA),
        const(KV_LORA, MLA_HEADS * HEAD_PAD),
        const(MLA_WIDTH, KV_LORA),
        const(1, 2 * d),
        tab, tab, tab, tab,
    ]
    dil_shapes = [jax.ShapeDtypeStruct((3, b, dl, s // dl, DIL_WIDTH), _bf16) for _, dl in DIL_GROUPS]
    dil_specs = [pl.BlockSpec((3, 1, dl, tm // dl, DIL_WIDTH), lambda bi, ti: (0, bi, 0, ti, 0))
                 for _, dl in DIL_GROUPS]
    out_shape = [
        jax.ShapeDtypeStruct((b, MLA_HEADS, s, HEAD_PAD), _bf16),
        jax.ShapeDtypeStruct((b, MLA_HEADS, s, HEAD_PAD), _bf16),
        jax.ShapeDtypeStruct((b, MLA_HEADS, nt, MLA_V, tm), _bf16),
        jax.ShapeDtypeStruct((b, s, MLA_WIDTH), _bf16),
        *dil_shapes,
        jax.ShapeDtypeStruct((b, s, DIL_WIDTH), _bf16),
        jax.ShapeDtypeStruct((b, s, 2 * d), _bf16),
    ]
    out_specs = [
        pl.BlockSpec((1, MLA_HEADS, tm, HEAD_PAD), lambda bi, ti: (bi, 0, ti, 0)),
        pl.BlockSpec((1, MLA_HEADS, tm, HEAD_PAD), lambda bi, ti: (bi, 0, ti, 0)),
        pl.BlockSpec((1, MLA_HEADS, 1, MLA_V, tm), lambda bi, ti: (bi, 0, ti, 0, 0)),
        pl.BlockSpec((1, tm, MLA_WIDTH), lambda bi, ti: (bi, ti, 0)),
        *dil_specs,
        pl.BlockSpec((1, tm, DIL_WIDTH), lambda bi, ti: (bi, ti, 0)),
        pl.BlockSpec((1, tm, 2 * d), lambda bi, ti: (bi, ti, 0)),
    ]
    kern = functools.partial(_in_kernel, q_scale=(MLA_NOPE + MLA_ROPE) ** -0.5 * math.log2(math.e))
    return pl.pallas_call(
        kern,
        grid=(b, nt),
        in_specs=in_specs,
        out_specs=out_specs,
        out_shape=out_shape,
        scratch_shapes=[pltpu.VMEM((d // LANES, tm, LANES), _f32)],
        compiler_params=pltpu.CompilerParams(vmem_limit_bytes=VMEM_LIMIT),
        name="in_proj",
    )(x, scale, shift, lw["g_norm"], lw["w_big"], lw["g_cq"], lw["w_uq"], lw["g_ckv"],
      lw["w_k"], lw["w_vt"], lw["b_gate"], tab_m, *tabs_d)


def _mla_kernel(q_ref, k_ref, vt_ref, o_ref, sa_ref, sb_ref, m_ref, acc_ref, *, n_tiles, tq,
                n_chunks, tk, unroll):
    ones = jnp.ones((16, tk), _bf16)
    per = tk // vt_ref.shape[-1]
    bufs = (sa_ref, sb_ref)

    def scores(qi, c, dst_ref):
        q = q_ref[0, 0, pl.ds(pl.multiple_of(qi * tq, tq), tq), :]
        kc = k_ref[0, 0, pl.ds(pl.multiple_of(c * tk, tk), tk), :]
        st = _dot_nt(kc, q)
        dst_ref[...] = st
        return jnp.max(st, axis=0, keepdims=True)

    def consume(c, src_ref, cmax):
        m_old = m_ref[...]
        m_new = jnp.maximum(m_old, cmax)
        alpha = jnp.exp2(m_old - m_new)
        p = jnp.exp2(src_ref[...] - m_new).astype(_bf16)
        vc = jnp.concatenate([vt_ref[0, 0, c * per + j] for j in range(per)], axis=1)
        va = jnp.concatenate([vc, ones], axis=0)
        acc_ref[...] = alpha * acc_ref[...] + _dot(va, p)
        m_ref[...] = m_new

    def steps(qi, first, count, cm):
        for u in range(count):
            cm_next = scores(qi, first + u + 1, bufs[(u + 1) % 2])
            consume(first + u, bufs[u % 2], cm)
            cm = cm_next
        return cm

    def tile(qi, cm):
        m_ref[...] = jnp.full(m_ref.shape, NEG_INF, _f32)
        acc_ref[...] = jnp.zeros(acc_ref.shape, _f32)
        cm = lax.fori_loop(0, n_chunks // unroll - 1,
                           lambda j, cm: steps(qi, j * unroll, unroll, cm), cm)
        cm = steps(qi, n_chunks - unroll, unroll - 1, cm)
        cm_next = scores(jnp.minimum(qi + 1, n_tiles - 1), 0, bufs[0])
        consume(n_chunks - 1, bufs[1], cm)
        acc = acc_ref[...]
        o_ref[0, 0, qi] = (acc[:MLA_V] / acc[MLA_V:MLA_V + 1]).astype(_bf16)
        return cm_next

    lax.fori_loop(0, n_tiles, tile, scores(0, 0, sa_ref))


def _mla_tiles(s):
    if s // MLA_K_CHUNK >= 4 * MLA_UNROLL:
        return MLA_Q_TILE, MLA_K_CHUNK, MLA_UNROLL
    return min(2 * MLA_Q_TILE, s), min(MLA_K_CHUNK // 2, s // 2), 2


def _mla_call(q, k, vt):
    b, hh, s, _ = q.shape
    nv, tv = vt.shape[2], vt.shape[4]
    tq, tk, unroll = _mla_tiles(s)
    n_chunks, n_tiles = s // tk, s // tq
    assert n_chunks % unroll == 0 and unroll % 2 == 0 and tk % tv == 0 and tq % tv == 0
    kern = functools.partial(_mla_kernel, n_tiles=n_tiles, tq=tq, n_chunks=n_chunks, tk=tk,
                             unroll=unroll)
    return pl.pallas_call(
        kern,
        grid=(b, hh),
        in_specs=[
            pl.BlockSpec((1, 1, s, HEAD_PAD), lambda bi, hi: (bi, hi, 0, 0)),
            pl.BlockSpec((1, 1, s, HEAD_PAD), lambda bi, hi: (bi, hi, 0, 0)),
            pl.BlockSpec((1, 1, nv, MLA_V, tv), lambda bi, hi: (bi, hi, 0, 0, 0)),
        ],
        out_specs=pl.BlockSpec((1, 1, n_tiles, MLA_V, tq), lambda bi, hi: (bi, hi, 0, 0, 0)),
        out_shape=jax.ShapeDtypeStruct((b, hh, n_tiles, MLA_V, tq), _bf16),
        scratch_shapes=[pltpu.VMEM((tk, tq), _f32), pltpu.VMEM((tk, tq), _f32),
                        pltpu.VMEM((1, tq), _f32), pltpu.VMEM((MLA_V + 16, tq), _f32)],
        compiler_params=pltpu.CompilerParams(vmem_limit_bytes=VMEM_LIMIT),
        name="mla_attn",
    )(q, k, vt)


def _band_kernel(q_ref, kp_ref, kc_ref, kn_ref, vp_ref, vc_ref, vn_ref, o_ref, stat_ref,
                 *, dil, tq, seq):
    i = pl.program_id(1)
    nk = BAND_SUB + 2 * BAND_HALF
    row = lax.broadcasted_iota(jnp.int32, (BAND_SUB, nk), 0)
    col = lax.broadcasted_iota(jnp.int32, (BAND_SUB, nk), 1)
    band = jnp.abs(col - BAND_HALF - row) <= BAND_HALF
    lane = lax.broadcasted_iota(jnp.int32, (1, LANES), 1)
    first_head = (lane % 64) < 32
    low_half = lane < 64

    def residue(r, carry):
        q = q_ref[0, 0, r]
        kk = jnp.concatenate([kp_ref[0, 0, r], kc_ref[0, 0, r], kn_ref[0, 0, r]], axis=0)
        vv = jnp.concatenate([vp_ref[0, 0, r], vc_ref[0, 0, r], vn_ref[0, 0, r]], axis=0)
        for sb in range(tq // BAND_SUB):
            a = sb * BAND_SUB
            if 0 < sb < tq // BAND_SUB - 1:
                valid = band
            else:
                kidx = i * tq + (a - BAND_HALF) + col
                valid = band & (kidx >= 0) & (kidx < seq)
            dst = pl.ds(r + a * dil, BAND_SUB, stride=dil) if dil > 1 else pl.ds(a, BAND_SUB)
            stats = [None, None]
            for p in range(DIL_SLABS):
                sl = slice(p * LANES, (p + 1) * LANES)
                qp = q[a:a + BAND_SUB, sl]
                kp = kk[a:a + nk, sl]
                vp = vv[a:a + nk, sl]
                res = []
                for sel in (first_head, jnp.logical_not(first_head)):
                    qm = jnp.where(sel, qp, jnp.zeros_like(qp))
                    sc = jnp.where(valid, _dot_nt(qm, kp), NEG_INF)
                    m = jnp.max(sc, axis=-1, keepdims=True)
                    e = jnp.exp2(sc - m)
                    l = jnp.sum(e, axis=-1, keepdims=True)
                    res.append((_dot(e.astype(_bf16), vp), m, l))
                o_ref[0, p, dst, :] = jnp.where(low_half, res[0][0], res[1][0])
                even = lane % 32 < LSE_LANES
                here = lane // 32 == p
                for j in range(2):
                    pair = jnp.where(even, res[0][1 + j], res[1][1 + j])
                    stats[j] = pair if p == 0 else jnp.where(here, pair, stats[j])
            stat_ref[0, 0, dst, :] = stats[0]
            stat_ref[0, 1, dst, :] = stats[1]
        return carry

    lax.fori_loop(0, dil, residue, 0, unroll=min(dil, 16))


def _band_call(qkv, dil):
    _, b, _, seq, w = qkv.shape
    s = seq * dil
    tq = max(BAND_SUB, BAND_ROWS // dil)
    assert tq % BAND_SUB == 0 and seq % tq == 0
    hb = tq // BAND_HALF
    last = seq // BAND_HALF - 1

    def cur(j):
        return pl.BlockSpec((1, 1, dil, tq, w), lambda bi, i: (j, bi, 0, i, 0))

    def prev(j):
        return pl.BlockSpec((1, 1, dil, BAND_HALF, w),
                            lambda bi, i: (j, bi, 0, jnp.maximum(i * hb - 1, 0), 0))

    def nxt(j):
        return pl.BlockSpec((1, 1, dil, BAND_HALF, w),
                            lambda bi, i: (j, bi, 0, jnp.minimum((i + 1) * hb, last), 0))

    kern = functools.partial(_band_kernel, dil=dil, tq=tq, seq=seq)
    return pl.pallas_call(
        kern,
        grid=(b, seq // tq),
        in_specs=[cur(0), prev(1), cur(1), nxt(1), prev(2), cur(2), nxt(2)],
        out_specs=[pl.BlockSpec((1, DIL_SLABS, dil * tq, LANES), lambda bi, i: (bi, 0, i, 0)),
                   pl.BlockSpec((1, 2, dil * tq, LANES), lambda bi, i: (bi, 0, i, 0))],
        out_shape=[jax.ShapeDtypeStruct((b, DIL_SLABS, s, LANES), _f32),
                   jax.ShapeDtypeStruct((b, 2, s, LANES), _f32)],
        compiler_params=pltpu.CompilerParams(vmem_limit_bytes=VMEM_LIMIT),
        name=f"band_attn_d{dil}",
    )(qkv, qkv, qkv, qkv, qkv, qkv, qkv)


def _out_kernel(x_ref, ot_ref, zm_ref, o0_ref, o1_ref, o2_ref, l0_ref, l1_ref, l2_ref,
                zd_ref, gt_ref, gate_ref, wpa_ref, wpb_ref, wo_ref, gf_ref, ex_ref, y_ref,
                *, final_norm):
    d = x_ref.shape[-1]
    ot = jnp.concatenate([ot_ref[0, hd, 0] for hd in range(MLA_HEADS)], axis=0)
    o_mla = jnp.transpose(ot.astype(_f32)) * zm_ref[0].astype(_f32)
    a = _dot(o_mla.astype(_bf16), wpa_ref[...])

    m0, m1, m2 = l0_ref[0, 0], l1_ref[0, 0], l2_ref[0, 0]
    mx = jnp.maximum(jnp.maximum(m0, m1), m2)
    w0, w1, w2 = jnp.exp2(m0 - mx), jnp.exp2(m1 - mx), jnp.exp2(m2 - mx)
    inv = 1.0 / (w0 * l0_ref[0, 1] + w1 * l1_ref[0, 1] + w2 * l2_ref[0, 1])
    o_dil = None
    for w, o_ref in ((w0, o0_ref), (w1, o1_ref), (w2, o2_ref)):
        wide = _dot((w * inv).astype(_bf16), ex_ref[...])
        term = wide * jnp.concatenate([o_ref[0, p] for p in range(DIL_SLABS)], axis=1)
        o_dil = term if o_dil is None else o_dil + term
    o_dil = o_dil * zd_ref[0].astype(_f32)
    bb = _dot(o_dil.astype(_bf16), wpb_ref[...])

    gt = gt_ref[0].astype(_f32)
    u = gt[:, :d] * a + gt[:, d:] * bb
    y = x_ref[0] + gate_ref[0] * _dot(u.astype(_bf16), wo_ref[...])
    if final_norm:
        y = _rms(y, gf_ref[...])
    y_ref[0] = y


def _out_call(x, ot, zm, outs, lses, zd, gt, gate, lw, g_final, final_norm):
    b, s, d = x.shape
    tm = min(TOKEN_TILE, s)
    tok = lambda w: pl.BlockSpec((1, tm, w), lambda bi, ti: (bi, ti, 0))
    slab = pl.BlockSpec((1, DIL_SLABS, tm, LANES), lambda bi, ti: (bi, 0, ti, 0))
    stat = pl.BlockSpec((1, 2, tm, LANES), lambda bi, ti: (bi, 0, ti, 0))
    const = lambda *shape: pl.BlockSpec(shape, lambda bi, ti: (0,) * len(shape))
    kern = functools.partial(_out_kernel, final_norm=final_norm)
    assert ot.shape[-1] % tm == 0
    per = ot.shape[-1] // tm
    spread = np.zeros((LANES, DIL_WIDTH), np.float32)
    for hd in range(DIL_HEADS):
        spread[hd * LSE_LANES, hd * DIL_HEAD_DIM:(hd + 1) * DIL_HEAD_DIM] = 1.0
    return pl.pallas_call(
        kern,
        grid=(b, s // tm),
        in_specs=[tok(d),
                  pl.BlockSpec((1, MLA_HEADS, 1, MLA_V, tm),
                               lambda bi, ti: (bi, 0, ti // per, 0, ti % per)),
                  tok(MLA_WIDTH),
                  slab, slab, slab, stat, stat, stat,
                  tok(DIL_WIDTH), tok(2 * d),
                  pl.BlockSpec((1, 1, d), lambda bi, ti: (bi, 0, 0)),
                  const(MLA_WIDTH, d), const(DIL_WIDTH, d), const(d, d), const(1, d),
                  const(LANES, DIL_WIDTH)],
        out_specs=tok(d),
        out_shape=jax.ShapeDtypeStruct((b, s, d), _f32),
        compiler_params=pltpu.CompilerParams(vmem_limit_bytes=VMEM_LIMIT),
        name="out_proj",
    )(x, ot, zm, outs[0], outs[1], outs[2], lses[0], lses[1], lses[2],
      zd, gt, gate, lw["w_pa"], lw["w_pb"], lw["w_out"], g_final, jnp.asarray(spread, _bf16))


def _mla_lane_order():
    half = MLA_ROPE // 2
    src = np.full((HEAD_PAD,), -1, np.int64)
    src[0:half] = MLA_NOPE + np.arange(half)
    src[half:64] = np.arange(64 - half)
    src[64:64 + half] = MLA_NOPE + half + np.arange(half)
    src[64 + half:64 + half + (MLA_NOPE - (64 - half))] = np.arange(64 - half, MLA_NOPE)
    return src


def _gather_cols(w, src):
    picked = jnp.take(w, jnp.asarray(np.maximum(src, 0)), axis=1)
    return jnp.where(jnp.asarray(src >= 0)[None, :], picked, 0.0)


def _dil_perm():
    idx = []
    half = DIL_HEAD_DIM // 2
    for p in range(DIL_HEADS // 2):
        ha, hb = 2 * p * DIL_HEAD_DIM, (2 * p + 1) * DIL_HEAD_DIM
        idx += list(range(ha, ha + half)) + list(range(hb, hb + half))
        idx += list(range(ha + half, ha + 2 * half)) + list(range(hb + half, hb + 2 * half))
    return np.asarray(idx, np.int64)


def _pack_layer(w_in, g_norm, b_gate, g_cq, w_uq, g_ckv, w_ukv, w_pa, w_pb, w_out):
    d = w_in.shape[0]
    splits = np.cumsum([Q_LORA, KV_LORA, MLA_ROPE, MLA_WIDTH] + [DIL_WIDTH] * (3 * N_GROUPS)
                       + [DIL_WIDTH])
    parts = jnp.split(w_in, list(splits), axis=1)
    perm = _dil_perm()
    lane_src = _mla_lane_order()
    kr_src = np.where(lane_src >= MLA_NOPE, lane_src - MLA_NOPE, -1)
    cols = [parts[0], parts[1], _gather_cols(parts[2], kr_src), parts[3]]
    for j in range(3 * N_GROUPS):
        pj = parts[4 + j]
        cols.append(pj if j % 3 == 2 else jnp.take(pj, jnp.asarray(perm), axis=1))
    cols += [parts[4 + 3 * N_GROUPS], parts[5 + 3 * N_GROUPS]]
    w_big = jnp.concatenate(cols, axis=1).astype(_bf16)

    dq = MLA_NOPE + MLA_ROPE
    q_src = np.concatenate([np.where(lane_src >= 0, h * dq + lane_src, -1) for h in range(MLA_HEADS)])
    w_uq_p = _gather_cols(w_uq, q_src).astype(_bf16)
    dkv = MLA_NOPE + MLA_V
    nope_src = np.where((lane_src >= 0) & (lane_src < MLA_NOPE), lane_src, -1)
    k_src = np.concatenate([np.where(nope_src >= 0, h * dkv + nope_src, -1) for h in range(MLA_HEADS)])
    w_k = _gather_cols(w_ukv, k_src).astype(_bf16)
    v_src = np.concatenate([h * dkv + MLA_NOPE + np.arange(MLA_V) for h in range(MLA_HEADS)])
    w_vt = jnp.transpose(jnp.take(w_ukv, jnp.asarray(v_src), axis=1)).astype(_bf16)

    return dict(
        w_big=w_big, g_norm=g_norm.reshape(1, d), b_gate=b_gate.reshape(1, -1),
        g_cq=g_cq.reshape(1, -1), w_uq=w_uq_p, g_ckv=g_ckv.reshape(1, -1), w_k=w_k,
        w_vt=w_vt,
        w_pa=w_pa.astype(_bf16), w_pb=w_pb.astype(_bf16), w_out=w_out.astype(_bf16))


def _rope_tables(s, tm):
    pos = np.arange(s, dtype=np.float64)[:, None]
    lane = np.arange(LANES)

    def table(freq_idx, active, n_freq):
        inv = np.power(ROPE_THETA, -2.0 * freq_idx / (2 * n_freq))
        ang = pos * inv[None, :]
        cos = np.where(active[None, :], np.cos(ang), 1.0)
        sign = np.where(lane < 64, -1.0, 1.0)
        sin = np.where(active[None, :], np.sin(ang) * sign[None, :], 0.0)
        return np.stack([cos, sin]).astype(np.float32)

    half = MLA_ROPE // 2
    tab_m = table((lane % 64).clip(0, half - 1).astype(np.float64), (lane % 64) < half, half)
    tab_d = jnp.asarray(table((lane % 32).astype(np.float64), np.ones(LANES, bool), DIL_HEAD_DIM // 2))
    tabs = []
    for _, dil in DIL_GROUPS:
        t = tab_d.reshape(2, s // tm, tm // dil, dil, LANES)
        tabs.append(jnp.swapaxes(t, 2, 3).reshape(2, s, LANES))
    return jnp.asarray(tab_m), tabs


def _trunk(x, mods, layers, g_final):
    b, s, d = x.shape
    assert s % (DIL_GROUPS[-1][1] * BAND_SUB) == 0 and s % TOKEN_TILE == 0
    tab_m, tabs_d = _rope_tables(s, min(TOKEN_TILE, s))
    depth = len(layers)
    for l, lw in enumerate(layers):
        mod = mods[l]
        shift, scale, gate = [mod[:, None, j * d:(j + 1) * d] for j in range(3)]
        q, k, vt, zm, d1, d4, d16, zd, gt = _in_call(x, scale, shift, lw, tab_m, tabs_d)
        ot = _mla_call(q, k, vt)
        outs, lses = [], []
        for qkv, (_, dl) in zip((d1, d4, d16), DIL_GROUPS):
            o_g, lse_g = _band_call(qkv, dl)
            outs.append(o_g)
            lses.append(lse_g)
        x = _out_call(x, ot, zm, outs, lses, zd, gt, gate, lw, g_final.reshape(1, d),
                      final_norm=(l == depth - 1))
    return x


def kernel(x_prompt, x_sample, c_prompt, c_sample, w_ada, b_ada, g_norm, w_in, b_gate, g_cq, w_uq,
           g_ckv, w_ukv, w_pa, w_pb, w_out, g_final):
    depth = w_in.shape[0]
    bp, bs = c_prompt.shape[0], c_sample.shape[0]
    rows = -(-(bp + bs) // 8) * 8
    c_all = jnp.concatenate([c_prompt, c_sample], axis=0)
    c_all = jnp.pad(c_all, ((0, rows - bp - bs), (0, 0)))
    mods = _ada_call(c_all, w_ada, b_ada)
    layers = [_pack_layer(w_in[l], g_norm[l], b_gate[l], g_cq[l], w_uq[l], g_ckv[l], w_ukv[l],
                          w_pa[l], w_pb[l], w_out[l]) for l in range(depth)]
    y_prompt = _trunk(x_prompt, mods[:, :bp], layers, g_final)
    y_sample = _trunk(x_sample, mods[:, bp:bp + bs], layers, g_final)
    return (y_prompt, y_sample)
```

```python
import functools
import math

import numpy as np
import jax
import jax.numpy as jnp
from jax import lax
from jax.experimental import pallas as pl
from jax.experimental.pallas import tpu as pltpu

ROPE_THETA = 10000.0
EPS = 1e-6
NEG_INF = -1e30

MLA_HEADS = 8
MLA_NOPE = 64
MLA_ROPE = 32
MLA_V = 64
Q_LORA = 384
KV_LORA = 256
MLA_WIDTH = MLA_HEADS * MLA_V
DIL_GROUPS = ((128, 1), (512, 4), (2048, 16))
DIL_HEADS = 8
DIL_HEAD_DIM = 64
DIL_WIDTH = DIL_HEADS * DIL_HEAD_DIM
N_GROUPS = len(DIL_GROUPS)
BAND_HALF = 64

LANES = 128
KR_PAD = LANES
HEAD_PAD = LANES
DIL_SLABS = DIL_WIDTH // LANES
LSE_LANES = LANES // DIL_HEADS

OFF_CQ = 0
OFF_CKV = OFF_CQ + Q_LORA
OFF_KR = OFF_CKV + KV_LORA
OFF_ZM = OFF_KR + KR_PAD
OFF_DIL = OFF_ZM + MLA_WIDTH
OFF_ZD = OFF_DIL + 3 * N_GROUPS * DIL_WIDTH
OFF_MG = OFF_ZD + DIL_WIDTH
W_BIG = OFF_MG + 2 * 1024

TOKEN_TILE = 512
MLA_Q_TILE = 512
MLA_K_CHUNK = 1024
MLA_UNROLL = 4
BAND_SUB = 128
BAND_ROWS = 2048
VMEM_LIMIT = 56 * 1024 * 1024

_f32 = jnp.float32
_bf16 = jnp.bfloat16
_NT = (((1,), (1,)), ((), ()))


def _dot(a, b):
    return jnp.dot(a, b, preferred_element_type=_f32)


def _dot_nt(a, b):
    return lax.dot_general(a, b, _NT, preferred_element_type=_f32)


def _rms(x, g):
    return x * lax.rsqrt(jnp.mean(x * x, axis=-1, keepdims=True) + EPS) * g


def _sigmoid(x):
    return 1.0 / (1.0 + jnp.exp(-x))


def _rope_pairs(x, cos, sin):
    return x * cos + pltpu.roll(x, 64, axis=1) * sin


def _ada_kernel(c_ref, w_ref, b_ref, o_ref):
    c = c_ref[...]
    a = c * _sigmoid(c)
    a_hi = a.astype(_bf16)
    a_lo = (a - a_hi.astype(_f32)).astype(_bf16)
    w = w_ref[0]
    w_hi = w.astype(_bf16)
    w_lo = (w - w_hi.astype(_f32)).astype(_bf16)
    o_ref[0] = _dot(a_hi, w_hi) + _dot(a_hi, w_lo) + _dot(a_lo, w_hi) + b_ref[0]


def _ada_call(c_all, w_ada, b_ada):
    depth, d, n = w_ada.shape
    rows = c_all.shape[0]
    nb = n // d
    return pl.pallas_call(
        _ada_kernel,
        grid=(depth, nb),
        in_specs=[
            pl.BlockSpec((rows, d), lambda l, j: (0, 0)),
            pl.BlockSpec((1, d, d), lambda l, j: (l, 0, j)),
            pl.BlockSpec((1, 1, d), lambda l, j: (l, 0, j)),
        ],
        out_specs=pl.BlockSpec((1, rows, d), lambda l, j: (l, 0, j)),
        out_shape=jax.ShapeDtypeStruct((depth, rows, n), _f32),
        compiler_params=pltpu.CompilerParams(vmem_limit_bytes=VMEM_LIMIT),
        name="adaln",
    )(c_all, w_ada, b_ada.reshape(depth, 1, n))


def _in_kernel(x_ref, sc_ref, sh_ref, gn_ref, w_ref, gcq_ref, wuq_ref, gckv_ref, wk_ref,
               wvt_ref, bg_ref, tm_ref, tmt_ref, t1_ref, t4_ref, t16_ref,
               q_ref, k_ref, vt_ref, zm_ref, d1_ref, d4_ref, d16_ref, zd_ref, gt_ref,
               hs_ref, *, q_scale):
    x = x_ref[0]
    rows, d_model = x.shape
    h = _rms(x, gn_ref[...]) * (1.0 + sc_ref[0]) + sh_ref[0]
    hb = h.astype(_bf16)
    for c in range(d_model // LANES):
        hs_ref[c] = h[:, c * LANES:(c + 1) * LANES]
    cos_m, sin_m = tm_ref[0], tm_ref[1]

    cq = _dot(hb, w_ref[:, OFF_CQ:OFF_CQ + Q_LORA])
    cqn = _rms(cq, gcq_ref[...]).astype(_bf16)
    qt = _dot_nt(wuq_ref[...], cqn)
    cos_t, sin_t = tmt_ref[0], tmt_ref[1]
    for hd in range(MLA_HEADS):
        xh = qt[hd * HEAD_PAD:(hd + 1) * HEAD_PAD, :]
        partner = jnp.concatenate([xh[HEAD_PAD // 2:], xh[:HEAD_PAD // 2]], axis=0)
        q_ref[0, hd, 0] = ((xh * cos_t + partner * sin_t) * q_scale).astype(_bf16)

    ckv = _dot(hb, w_ref[:, OFF_CKV:OFF_CKV + KV_LORA])
    ckvn = _rms(ckv, gckv_ref[...]).astype(_bf16)
    kr = _dot(hb, w_ref[:, OFF_KR:OFF_KR + KR_PAD])
    k = _dot(ckvn, wk_ref[...])
    for hd in range(MLA_HEADS):
        sl = slice(hd * HEAD_PAD, (hd + 1) * HEAD_PAD)
        k_ref[0, hd] = _rope_pairs(k[:, sl] + kr, cos_m, sin_m).astype(_bf16)
    vt = _dot_nt(wvt_ref[...], ckvn)
    for hd in range(MLA_HEADS):
        vt_ref[0, hd, 0] = vt[hd * MLA_V:(hd + 1) * MLA_V, :].astype(_bf16)

    zm = _dot(hb, w_ref[:, OFF_ZM:OFF_ZM + MLA_WIDTH])
    zm_ref[0] = (zm * _sigmoid(zm)).astype(_bf16)

    zd = _dot(hb, w_ref[:, OFF_ZD:OFF_ZD + DIL_WIDTH])
    zd_ref[0] = (zd * _sigmoid(zd)).astype(_bf16)

    mg = _dot(hb, w_ref[:, OFF_MG:W_BIG]) + bg_ref[...]
    gt_ref[0] = _sigmoid(mg).astype(_bf16)

    for g, (dil, out_ref, tab_ref) in enumerate(((1, d1_ref, t1_ref), (4, d4_ref, t4_ref),
                                                 (16, d16_ref, t16_ref))):
        per = rows // dil
        if dil == 1:
            hg = hb
        else:
            hg = jnp.concatenate(
                [jnp.concatenate([hs_ref[c, pl.ds(r, per, stride=dil), :] for r in range(dil)], axis=0)
                 for c in range(d_model // LANES)], axis=1).astype(_bf16)
        cos_d, sin_d = tab_ref[0], tab_ref[1]
        for kind in range(3):
            off = OFF_DIL + (3 * g + kind) * DIL_WIDTH
            u = _dot(hg, w_ref[:, off:off + DIL_WIDTH])
            if kind == 2:
                ub = u.astype(_bf16)
            else:
                post = DIL_HEAD_DIM ** -0.5 * math.log2(math.e) if kind == 0 else 1.0
                ub = jnp.concatenate(
                    [(_rope_pairs(u[:, p * LANES:(p + 1) * LANES], cos_d, sin_d) * post).astype(_bf16)
                     for p in range(DIL_SLABS)], axis=1)
            for r in range(dil):
                out_ref[kind, 0, r] = ub[r * per:(r + 1) * per]


def _in_call(x, scale, shift, lw, tab_m, tabs_d):
    b, s, d = x.shape
    tm = min(TOKEN_TILE, s)
    nt = s // tm
    const = lambda *shape: pl.BlockSpec(shape, lambda bi, ti: (0,) * len(shape),
                                        pipeline_mode=pl.Buffered(1))
    tab = pl.BlockSpec((2, tm, LANES), lambda bi, ti: (0, ti, 0))
    in_specs = [
        pl.BlockSpec((1, tm, d), lambda bi, ti: (bi, ti, 0)),
        pl.BlockSpec((1, 1, d), lambda bi, ti: (bi, 0, 0)),
        pl.BlockSpec((1, 1, d), lambda bi, ti: (bi, 0, 0)),
        const(1, d),
        const(d, W_BIG),
        const(1, Q_LORA),
        const(MLA_HEADS * HEAD_PAD, Q_LORA),
        const(1, KV_LORA),
        const(KV_LORA, MLA_HEADS * HEAD_PAD),
        const(MLA_WIDTH, KV_LORA),
        const(1, 2 * d),
        tab, pl.BlockSpec((2, LANES, tm), lambda bi, ti: (0, 0, ti)), tab, tab, tab,
    ]
    dil_shapes = [jax.ShapeDtypeStruct((3, b, dl, s // dl, DIL_WIDTH), _bf16) for _, dl in DIL_GROUPS]
    dil_specs = [pl.BlockSpec((3, 1, dl, tm // dl, DIL_WIDTH), lambda bi, ti: (0, bi, 0, ti, 0))
                 for _, dl in DIL_GROUPS]
    out_shape = [
        jax.ShapeDtypeStruct((b, MLA_HEADS, nt, HEAD_PAD, tm), _bf16),
        jax.ShapeDtypeStruct((b, MLA_HEADS, s, HEAD_PAD), _bf16),
        jax.ShapeDtypeStruct((b, MLA_HEADS, nt, MLA_V, tm), _bf16),
        jax.ShapeDtypeStruct((b, s, MLA_WIDTH), _bf16),
        *dil_shapes,
        jax.ShapeDtypeStruct((b, s, DIL_WIDTH), _bf16),
        jax.ShapeDtypeStruct((b, s, 2 * d), _bf16),
    ]
    out_specs = [
        pl.BlockSpec((1, MLA_HEADS, 1, HEAD_PAD, tm), lambda bi, ti: (bi, 0, ti, 0, 0)),
        pl.BlockSpec((1, MLA_HEADS, tm, HEAD_PAD), lambda bi, ti: (bi, 0, ti, 0)),
        pl.BlockSpec((1, MLA_HEADS, 1, MLA_V, tm), lambda bi, ti: (bi, 0, ti, 0, 0)),
        pl.BlockSpec((1, tm, MLA_WIDTH), lambda bi, ti: (bi, ti, 0)),
        *dil_specs,
        pl.BlockSpec((1, tm, DIL_WIDTH), lambda bi, ti: (bi, ti, 0)),
        pl.BlockSpec((1, tm, 2 * d), lambda bi, ti: (bi, ti, 0)),
    ]
    kern = functools.partial(_in_kernel, q_scale=(MLA_NOPE + MLA_ROPE) ** -0.5 * math.log2(math.e))
    return pl.pallas_call(
        kern,
        grid=(b, nt),
        in_specs=in_specs,
        out_specs=out_specs,
        out_shape=out_shape,
        scratch_shapes=[pltpu.VMEM((d // LANES, tm, LANES), _f32)],
        compiler_params=pltpu.CompilerParams(vmem_limit_bytes=VMEM_LIMIT),
        name="in_proj",
    )(x, scale, shift, lw["g_norm"], lw["w_big"], lw["g_cq"], lw["w_uq"], lw["g_ckv"],
      lw["w_k"], lw["w_vt"], lw["b_gate"], tab_m, jnp.swapaxes(tab_m, 1, 2), *tabs_d)


def _mla_kernel(q_ref, k_ref, vt_ref, o_ref, sa_ref, sb_ref, m_ref, acc_ref, *, n_tiles, tq,
                n_chunks, tk, unroll):
    ones = jnp.ones((16, tk), _bf16)
    per = tk // vt_ref.shape[-1]
    per_q = tq // q_ref.shape[-1]
    bufs = (sa_ref, sb_ref)

    def scores(qi, c, dst_ref):
        qt = jnp.concatenate([q_ref[0, 0, qi * per_q + j] for j in range(per_q)], axis=1)
        kc = k_ref[0, 0, pl.ds(pl.multiple_of(c * tk, tk), tk), :]
        st = _dot(kc, qt)
        dst_ref[...] = st
        return jnp.max(st, axis=0, keepdims=True)

    def consume(c, src_ref, cmax):
        m_old = m_ref[...]
        m_new = jnp.maximum(m_old, cmax)
        alpha = jnp.exp2(m_old - m_new)
        p = jnp.exp2(src_ref[...] - m_new).astype(_bf16)
        vc = jnp.concatenate([vt_ref[0, 0, c * per + j] for j in range(per)], axis=1)
        va = jnp.concatenate([vc, ones], axis=0)
        acc_ref[...] = alpha * acc_ref[...] + _dot(va, p)
        m_ref[...] = m_new

    def steps(qi, first, count, cm):
        for u in range(count):
            cm_next = scores(qi, first + u + 1, bufs[(u + 1) % 2])
            consume(first + u, bufs[u % 2], cm)
            cm = cm_next
        return cm

    def tile(qi, cm):
        m_ref[...] = jnp.full(m_ref.shape, NEG_INF, _f32)
        acc_ref[...] = jnp.zeros(acc_ref.shape, _f32)
        cm = lax.fori_loop(0, n_chunks // unroll - 1,
                           lambda j, cm: steps(qi, j * unroll, unroll, cm), cm)
        cm = steps(qi, n_chunks - unroll, unroll - 1, cm)
        cm_next = scores(jnp.minimum(qi + 1, n_tiles - 1), 0, bufs[0])
        consume(n_chunks - 1, bufs[1], cm)
        acc = acc_ref[...]
        o_ref[0, 0, qi] = (acc[:MLA_V] / acc[MLA_V:MLA_V + 1]).astype(_bf16)
        return cm_next

    lax.fori_loop(0, n_tiles, tile, scores(0, 0, sa_ref))


def _mla_tiles(s):
    if s // MLA_K_CHUNK >= 4 * MLA_UNROLL:
        return MLA_Q_TILE, MLA_K_CHUNK, MLA_UNROLL
    return min(2 * MLA_Q_TILE, s), min(MLA_K_CHUNK // 2, s // 2), 2


def _mla_call(q, k, vt):
    b, hh, s, _ = k.shape
    nv, tv = vt.shape[2], vt.shape[4]
    tq, tk, unroll = _mla_tiles(s)
    n_chunks, n_tiles = s // tk, s // tq
    assert n_chunks % unroll == 0 and unroll % 2 == 0 and tk % tv == 0 and tq % tv == 0
    kern = functools.partial(_mla_kernel, n_tiles=n_tiles, tq=tq, n_chunks=n_chunks, tk=tk,
                             unroll=unroll)
    return pl.pallas_call(
        kern,
        grid=(b, hh),
        in_specs=[
            pl.BlockSpec((1, 1, nv, HEAD_PAD, tv), lambda bi, hi: (bi, hi, 0, 0, 0)),
            pl.BlockSpec((1, 1, s, HEAD_PAD), lambda bi, hi: (bi, hi, 0, 0)),
            pl.BlockSpec((1, 1, nv, MLA_V, tv), lambda bi, hi: (bi, hi, 0, 0, 0)),
        ],
        out_specs=pl.BlockSpec((1, 1, n_tiles, MLA_V, tq), lambda bi, hi: (bi, hi, 0, 0, 0)),
        out_shape=jax.ShapeDtypeStruct((b, hh, n_tiles, MLA_V, tq), _bf16),
        scratch_shapes=[pltpu.VMEM((tk, tq), _f32), pltpu.VMEM((tk, tq), _f32),
                        pltpu.VMEM((1, tq), _f32), pltpu.VMEM((MLA_V + 16, tq), _f32)],
        compiler_params=pltpu.CompilerParams(vmem_limit_bytes=VMEM_LIMIT),
        name="mla_attn",
    )(q, k, vt)


def _band_kernel(q_ref, kp_ref, kc_ref, kn_ref, vp_ref, vc_ref, vn_ref, o_ref, stat_ref,
                 *, dil, tq, seq):
    i = pl.program_id(1)
    nk = BAND_SUB + 2 * BAND_HALF
    row = lax.broadcasted_iota(jnp.int32, (BAND_SUB, nk), 0)
    col = lax.broadcasted_iota(jnp.int32, (BAND_SUB, nk), 1)
    band = jnp.abs(col - BAND_HALF - row) <= BAND_HALF
    lane = lax.broadcasted_iota(jnp.int32, (1, LANES), 1)
    first_head = (lane % 64) < 32
    low_half = lane < 64

    def residue(r, carry):
        q = q_ref[0, 0, r]
        kk = jnp.concatenate([kp_ref[0, 0, r], kc_ref[0, 0, r], kn_ref[0, 0, r]], axis=0)
        vv = jnp.concatenate([vp_ref[0, 0, r], vc_ref[0, 0, r], vn_ref[0, 0, r]], axis=0)
        for sb in range(tq // BAND_SUB):
            a = sb * BAND_SUB
            if 0 < sb < tq // BAND_SUB - 1:
                valid = band
            else:
                kidx = i * tq + (a - BAND_HALF) + col
                valid = band & (kidx >= 0) & (kidx < seq)
            dst = pl.ds(r + a * dil, BAND_SUB, stride=dil) if dil > 1 else pl.ds(a, BAND_SUB)
            stats = [None, None]
            for p in range(DIL_SLABS):
                sl = slice(p * LANES, (p + 1) * LANES)
                qp = q[a:a + BAND_SUB, sl]
                kp = kk[a:a + nk, sl]
                vp = vv[a:a + nk, sl]
                res = []
                for sel in (first_head, jnp.logical_not(first_head)):
                    qm = jnp.where(sel, qp, jnp.zeros_like(qp))
                    sc = jnp.where(valid, _dot_nt(qm, kp), NEG_INF)
                    m = jnp.max(sc, axis=-1, keepdims=True)
                    e = jnp.exp2(sc - m)
                    l = jnp.sum(e, axis=-1, keepdims=True)
                    res.append((_dot(e.astype(_bf16), vp), m, l))
                o_ref[0, p, dst, :] = jnp.where(low_half, res[0][0], res[1][0])
                even = lane % 32 < LSE_LANES
                here = lane // 32 == p
                for j in range(2):
                    pair = jnp.where(even, res[0][1 + j], res[1][1 + j])
                    stats[j] = pair if p == 0 else jnp.where(here, pair, stats[j])
            stat_ref[0, 0, dst, :] = stats[0]
            stat_ref[0, 1, dst, :] = stats[1]
        return carry

    lax.fori_loop(0, dil, residue, 0, unroll=min(dil, 16))


def _band_call(qkv, dil):
    _, b, _, seq, w = qkv.shape
    s = seq * dil
    tq = max(BAND_SUB, BAND_ROWS // dil)
    assert tq % BAND_SUB == 0 and seq % tq == 0
    hb = tq // BAND_HALF
    last = seq // BAND_HALF - 1

    def cur(j):
        return pl.BlockSpec((1, 1, dil, tq, w), lambda bi, i: (j, bi, 0, i, 0))

    def prev(j):
        return pl.BlockSpec((1, 1, dil, BAND_HALF, w),
                            lambda bi, i: (j, bi, 0, jnp.maximum(i * hb - 1, 0), 0))

    def nxt(j):
        return pl.BlockSpec((1, 1, dil, BAND_HALF, w),
                            lambda bi, i: (j, bi, 0, jnp.minimum((i + 1) * hb, last), 0))

    kern = functools.partial(_band_kernel, dil=dil, tq=tq, seq=seq)
    return pl.pallas_call(
        kern,
        grid=(b, seq // tq),
        in_specs=[cur(0), prev(1), cur(1), nxt(1), prev(2), cur(2), nxt(2)],
        out_specs=[pl.BlockSpec((1, DIL_SLABS, dil * tq, LANES), lambda bi, i: (bi, 0, i, 0)),
                   pl.BlockSpec((1, 2, dil * tq, LANES), lambda bi, i: (bi, 0, i, 0))],
        out_shape=[jax.ShapeDtypeStruct((b, DIL_SLABS, s, LANES), _f32),
                   jax.ShapeDtypeStruct((b, 2, s, LANES), _f32)],
        compiler_params=pltpu.CompilerParams(vmem_limit_bytes=VMEM_LIMIT),
        name=f"band_attn_d{dil}",
    )(qkv, qkv, qkv, qkv, qkv, qkv, qkv)


def _out_kernel(x_ref, ot_ref, zm_ref, o0_ref, o1_ref, o2_ref, l0_ref, l1_ref, l2_ref,
                zd_ref, gt_ref, gate_ref, wpa_ref, wpb_ref, wo_ref, gf_ref, ex_ref, y_ref,
                *, final_norm):
    d = x_ref.shape[-1]
    ot = jnp.concatenate([ot_ref[0, hd, 0] for hd in range(MLA_HEADS)], axis=0)
    o_mla = jnp.transpose(ot.astype(_f32)) * zm_ref[0].astype(_f32)
    a = _dot(o_mla.astype(_bf16), wpa_ref[...])

    m0, m1, m2 = l0_ref[0, 0], l1_ref[0, 0], l2_ref[0, 0]
    mx = jnp.maximum(jnp.maximum(m0, m1), m2)
    w0, w1, w2 = jnp.exp2(m0 - mx), jnp.exp2(m1 - mx), jnp.exp2(m2 - mx)
    inv = 1.0 / (w0 * l0_ref[0, 1] + w1 * l1_ref[0, 1] + w2 * l2_ref[0, 1])
    o_dil = None
    for w, o_ref in ((w0, o0_ref), (w1, o1_ref), (w2, o2_ref)):
        wide = _dot((w * inv).astype(_bf16), ex_ref[...])
        term = wide * jnp.concatenate([o_ref[0, p] for p in range(DIL_SLABS)], axis=1)
        o_dil = term if o_dil is None else o_dil + term
    o_dil = o_dil * zd_ref[0].astype(_f32)
    bb = _dot(o_dil.astype(_bf16), wpb_ref[...])

    gt = gt_ref[0].astype(_f32)
    u = gt[:, :d] * a + gt[:, d:] * bb
    y = x_ref[0] + gate_ref[0] * _dot(u.astype(_bf16), wo_ref[...])
    if final_norm:
        y = _rms(y, gf_ref[...])
    y_ref[0] = y


def _out_call(x, ot, zm, outs, lses, zd, gt, gate, lw, g_final, final_norm):
    b, s, d = x.shape
    tm = min(TOKEN_TILE, s)
    tok = lambda w: pl.BlockSpec((1, tm, w), lambda bi, ti: (bi, ti, 0))
    slab = pl.BlockSpec((1, DIL_SLABS, tm, LANES), lambda bi, ti: (bi, 0, ti, 0))
    stat = pl.BlockSpec((1, 2, tm, LANES), lambda bi, ti: (bi, 0, ti, 0))
    const = lambda *shape: pl.BlockSpec(shape, lambda bi, ti: (0,) * len(shape))
    kern = functools.partial(_out_kernel, final_norm=final_norm)
    assert ot.shape[-1] % tm == 0
    per = ot.shape[-1] // tm
    spread = np.zeros((LANES, DIL_WIDTH), np.float32)
    for hd in range(DIL_HEADS):
        spread[hd * LSE_LANES, hd * DIL_HEAD_DIM:(hd + 1) * DIL_HEAD_DIM] = 1.0
    return pl.pallas_call(
        kern,
        grid=(b, s // tm),
        in_specs=[tok(d),
                  pl.BlockSpec((1, MLA_HEADS, 1, MLA_V, tm),
                               lambda bi, ti: (bi, 0, ti // per, 0, ti % per)),
                  tok(MLA_WIDTH),
                  slab, slab, slab, stat, stat, stat,
                  tok(DIL_WIDTH), tok(2 * d),
                  pl.BlockSpec((1, 1, d), lambda bi, ti: (bi, 0, 0)),
                  const(MLA_WIDTH, d), const(DIL_WIDTH, d), const(d, d), const(1, d),
                  const(LANES, DIL_WIDTH)],
        out_specs=tok(d),
        out_shape=jax.ShapeDtypeStruct((b, s, d), _f32),
        compiler_params=pltpu.CompilerParams(vmem_limit_bytes=VMEM_LIMIT),
        name="out_proj",
    )(x, ot, zm, outs[0], outs[1], outs[2], lses[0], lses[1], lses[2],
      zd, gt, gate, lw["w_pa"], lw["w_pb"], lw["w_out"], g_final, jnp.asarray(spread, _bf16))


def _mla_lane_order():
    half = MLA_ROPE // 2
    src = np.full((HEAD_PAD,), -1, np.int64)
    src[0:half] = MLA_NOPE + np.arange(half)
    src[half:64] = np.arange(64 - half)
    src[64:64 + half] = MLA_NOPE + half + np.arange(half)
    src[64 + half:64 + half + (MLA_NOPE - (64 - half))] = np.arange(64 - half, MLA_NOPE)
    return src


def _gather_cols(w, src):
    picked = jnp.take(w, jnp.asarray(np.maximum(src, 0)), axis=1)
    return jnp.where(jnp.asarray(src >= 0)[None, :], picked, 0.0)


def _dil_perm():
    idx = []
    half = DIL_HEAD_DIM // 2
    for p in range(DIL_HEADS // 2):
        ha, hb = 2 * p * DIL_HEAD_DIM, (2 * p + 1) * DIL_HEAD_DIM
        idx += list(range(ha, ha + half)) + list(range(hb, hb + half))
        idx += list(range(ha + half, ha + 2 * half)) + list(range(hb + half, hb + 2 * half))
    return np.asarray(idx, np.int64)


def _pack_layer(w_in, g_norm, b_gate, g_cq, w_uq, g_ckv, w_ukv, w_pa, w_pb, w_out):
    d = w_in.shape[0]
    splits = np.cumsum([Q_LORA, KV_LORA, MLA_ROPE, MLA_WIDTH] + [DIL_WIDTH] * (3 * N_GROUPS)
                       + [DIL_WIDTH])
    parts = jnp.split(w_in, list(splits), axis=1)
    perm = _dil_perm()
    lane_src = _mla_lane_order()
    kr_src = np.where(lane_src >= MLA_NOPE, lane_src - MLA_NOPE, -1)
    cols = [parts[0], parts[1], _gather_cols(parts[2], kr_src), parts[3]]
    for j in range(3 * N_GROUPS):
        pj = parts[4 + j]
        cols.append(pj if j % 3 == 2 else jnp.take(pj, jnp.asarray(perm), axis=1))
    cols += [parts[4 + 3 * N_GROUPS], parts[5 + 3 * N_GROUPS]]
    w_big = jnp.concatenate(cols, axis=1).astype(_bf16)

    dq = MLA_NOPE + MLA_ROPE
    q_src = np.concatenate([np.where(lane_src >= 0, h * dq + lane_src, -1) for h in range(MLA_HEADS)])
    w_uq_p = jnp.transpose(_gather_cols(w_uq, q_src)).astype(_bf16)
    dkv = MLA_NOPE + MLA_V
    nope_src = np.where((lane_src >= 0) & (lane_src < MLA_NOPE), lane_src, -1)
    k_src = np.concatenate([np.where(nope_src >= 0, h * dkv + nope_src, -1) for h in range(MLA_HEADS)])
    w_k = _gather_cols(w_ukv, k_src).astype(_bf16)
    v_src = np.concatenate([h * dkv + MLA_NOPE + np.arange(MLA_V) for h in range(MLA_HEADS)])
    w_vt = jnp.transpose(jnp.take(w_ukv, jnp.asarray(v_src), axis=1)).astype(_bf16)

    return dict(
        w_big=w_big, g_norm=g_norm.reshape(1, d), b_gate=b_gate.reshape(1, -1),
        g_cq=g_cq.reshape(1, -1), w_uq=w_uq_p, g_ckv=g_ckv.reshape(1, -1), w_k=w_k,
        w_vt=w_vt,
        w_pa=w_pa.astype(_bf16), w_pb=w_pb.astype(_bf16), w_out=w_out.astype(_bf16))


def _rope_tables(s, tm):
    pos = np.arange(s, dtype=np.float64)[:, None]
    lane = np.arange(LANES)

    def table(freq_idx, active, n_freq):
        inv = np.power(ROPE_THETA, -2.0 * freq_idx / (2 * n_freq))
        ang = pos * inv[None, :]
        cos = np.where(active[None, :], np.cos(ang), 1.0)
        sign = np.where(lane < 64, -1.0, 1.0)
        sin = np.where(active[None, :], np.sin(ang) * sign[None, :], 0.0)
        return np.stack([cos, sin]).astype(np.float32)

    half = MLA_ROPE // 2
    tab_m = table((lane % 64).clip(0, half - 1).astype(np.float64), (lane % 64) < half, half)
    tab_d = jnp.asarray(table((lane % 32).astype(np.float64), np.ones(LANES, bool), DIL_HEAD_DIM // 2))
    tabs = []
    for _, dil in DIL_GROUPS:
        t = tab_d.reshape(2, s // tm, tm // dil, dil, LANES)
        tabs.append(jnp.swapaxes(t, 2, 3).reshape(2, s, LANES))
    return jnp.asarray(tab_m), tabs


def _trunk(x, mods, layers, g_final):
    b, s, d = x.shape
    assert s % (DIL_GROUPS[-1][1] * BAND_SUB) == 0 and s % TOKEN_TILE == 0
    tab_m, tabs_d = _rope_tables(s, min(TOKEN_TILE, s))
    depth = len(layers)
    for l, lw in enumerate(layers):
        mod = mods[l]
        shift, scale, gate = [mod[:, None, j * d:(j + 1) * d] for j in range(3)]
        q, k, vt, zm, d1, d4, d16, zd, gt = _in_call(x, scale, shift, lw, tab_m, tabs_d)
        ot = _mla_call(q, k, vt)
        outs, lses = [], []
        for qkv, (_, dl) in zip((d1, d4, d16), DIL_GROUPS):
            o_g, lse_g = _band_call(qkv, dl)
            outs.append(o_g)
            lses.append(lse_g)
        x = _out_call(x, ot, zm, outs, lses, zd, gt, gate, lw, g_final.reshape(1, d),
                      final_norm=(l == depth - 1))
    return x


def kernel(x_prompt, x_sample, c_prompt, c_sample, w_ada, b_ada, g_norm, w_in, b_gate, g_cq, w_uq,
           g_ckv, w_ukv, w_pa, w_pb, w_out, g_final):
    depth = w_in.shape[0]
    bp, bs = c_prompt.shape[0], c_sample.shape[0]
    rows = -(-(bp + bs) // 8) * 8
    c_all = jnp.concatenate([c_prompt, c_sample], axis=0)
    c_all = jnp.pad(c_all, ((0, rows - bp - bs), (0, 0)))
    mods = _ada_call(c_all, w_ada, b_ada)
    layers = [_pack_layer(w_in[l], g_norm[l], b_gate[l], g_cq[l], w_uq[l], g_ckv[l], w_ukv[l],
                          w_pa[l], w_pb[l], w_out[l]) for l in range(depth)]
    y_prompt = _trunk(x_prompt, mods[:, :bp], layers, g_final)
    y_sample = _trunk(x_sample, mods[:, bp:bp + bs], layers, g_final)
    return (y_prompt, y_sample)
```

```python
import functools
import math

import numpy as np
import jax
import jax.numpy as jnp
from jax import lax
from jax.experimental import pallas as pl
from jax.experimental.pallas import tpu as pltpu

ROPE_THETA = 10000.0
EPS = 1e-6
NEG_INF = -1e30

MLA_HEADS = 8
MLA_NOPE = 64
MLA_ROPE = 32
MLA_V = 64
Q_LORA = 384
KV_LORA = 256
MLA_WIDTH = MLA_HEADS * MLA_V
DIL_GROUPS = ((128, 1), (512, 4), (2048, 16))
DIL_HEADS = 8
DIL_HEAD_DIM = 64
DIL_WIDTH = DIL_HEADS * DIL_HEAD_DIM
N_GROUPS = len(DIL_GROUPS)
BAND_HALF = 64

LANES = 128
BF16_ROWS = 16
HALF_LANES = LANES // 2
KR_PAD = LANES
HEAD_PAD = LANES
DIL_SLABS = DIL_WIDTH // LANES
LSE_LANES = LANES // DIL_HEADS

OFF_CQ = 0
OFF_CKV = OFF_CQ + Q_LORA
OFF_KR = OFF_CKV + KV_LORA
OFF_ZM = OFF_KR + KR_PAD
OFF_DIL = OFF_ZM + MLA_WIDTH
OFF_ZD = OFF_DIL + 3 * N_GROUPS * DIL_WIDTH
OFF_MG = OFF_ZD + DIL_WIDTH
W_BIG = OFF_MG + 2 * 1024

TOKEN_TILE = 512
MLA_Q_TILE = 512
MLA_K_CHUNK = 1024
MLA_UNROLL = 4
BAND_SUB = 128
BAND_ROWS = 2048
BAND_UNROLL = 16
VMEM_LIMIT = 56 * 1024 * 1024

_f32 = jnp.float32
_bf16 = jnp.bfloat16
_NT = (((1,), (1,)), ((), ()))


def _dot(a, b):
    return jnp.dot(a, b, preferred_element_type=_f32)


def _dot_nt(a, b):
    return lax.dot_general(a, b, _NT, preferred_element_type=_f32)


def _rms(x, g):
    return x * lax.rsqrt(jnp.mean(x * x, axis=-1, keepdims=True) + EPS) * g


def _sigmoid(x):
    return 1.0 / (1.0 + jnp.exp(-x))


def _rope_pairs(x, cos, sin):
    return x * cos + pltpu.roll(x, HALF_LANES, axis=1) * sin


def _ada_kernel(c_ref, w_ref, b_ref, o_ref):
    c = c_ref[...]
    a = c * _sigmoid(c)
    a_hi = a.astype(_bf16)
    a_lo = (a - a_hi.astype(_f32)).astype(_bf16)
    w = w_ref[0]
    w_hi = w.astype(_bf16)
    w_lo = (w - w_hi.astype(_f32)).astype(_bf16)
    o_ref[0] = _dot(a_hi, w_hi) + _dot(a_hi, w_lo) + _dot(a_lo, w_hi) + b_ref[0]


def _ada_call(c_all, w_ada, b_ada):
    depth, d, n = w_ada.shape
    rows = c_all.shape[0]
    nb = n // d
    return pl.pallas_call(
        _ada_kernel,
        grid=(depth, nb),
        in_specs=[
            pl.BlockSpec((rows, d), lambda l, j: (0, 0)),
            pl.BlockSpec((1, d, d), lambda l, j: (l, 0, j)),
            pl.BlockSpec((1, 1, d), lambda l, j: (l, 0, j)),
        ],
        out_specs=pl.BlockSpec((1, rows, d), lambda l, j: (l, 0, j)),
        out_shape=jax.ShapeDtypeStruct((depth, rows, n), _f32),
        compiler_params=pltpu.CompilerParams(vmem_limit_bytes=VMEM_LIMIT),
        name="adaln",
    )(c_all, w_ada, b_ada.reshape(depth, 1, n))


def _in_kernel(x_ref, sc_ref, sh_ref, gn_ref, w_ref, gcq_ref, wuq_ref, gckv_ref, wk_ref,
               wvt_ref, bg_ref, tm_ref, tmt_ref, t1_ref, t4_ref, t16_ref,
               q_ref, k_ref, vt_ref, zm_ref, d1_ref, d4_ref, d16_ref, zd_ref, gt_ref,
               hs_ref, *, q_scale):
    x = x_ref[0]
    rows, d_model = x.shape
    h = _rms(x, gn_ref[...]) * (1.0 + sc_ref[0]) + sh_ref[0]
    hb = h.astype(_bf16)
    for c in range(d_model // LANES):
        hs_ref[c] = h[:, c * LANES:(c + 1) * LANES]
    cos_m, sin_m = tm_ref[0], tm_ref[1]

    cq = _dot(hb, w_ref[:, OFF_CQ:OFF_CQ + Q_LORA])
    cqn = _rms(cq, gcq_ref[...]).astype(_bf16)
    qt = _dot_nt(wuq_ref[...], cqn)
    cos_t, sin_t = tmt_ref[0], tmt_ref[1]
    for hd in range(MLA_HEADS):
        xh = qt[hd * HEAD_PAD:(hd + 1) * HEAD_PAD, :]
        partner = jnp.concatenate([xh[HEAD_PAD // 2:], xh[:HEAD_PAD // 2]], axis=0)
        q_ref[0, hd, 0] = ((xh * cos_t + partner * sin_t) * q_scale).astype(_bf16)

    ckv = _dot(hb, w_ref[:, OFF_CKV:OFF_CKV + KV_LORA])
    ckvn = _rms(ckv, gckv_ref[...]).astype(_bf16)
    kr = _dot(hb, w_ref[:, OFF_KR:OFF_KR + KR_PAD])
    k = _dot(ckvn, wk_ref[...])
    for hd in range(MLA_HEADS):
        sl = slice(hd * HEAD_PAD, (hd + 1) * HEAD_PAD)
        k_ref[0, hd] = _rope_pairs(k[:, sl] + kr, cos_m, sin_m).astype(_bf16)
    vt = _dot_nt(wvt_ref[...], ckvn)
    for hd in range(MLA_HEADS):
        vt_ref[0, hd, 0] = vt[hd * MLA_V:(hd + 1) * MLA_V, :].astype(_bf16)

    zm = _dot(hb, w_ref[:, OFF_ZM:OFF_ZM + MLA_WIDTH])
    zm_ref[0] = (zm * _sigmoid(zm)).astype(_bf16)

    zd = _dot(hb, w_ref[:, OFF_ZD:OFF_ZD + DIL_WIDTH])
    zd_ref[0] = (zd * _sigmoid(zd)).astype(_bf16)

    mg = _dot(hb, w_ref[:, OFF_MG:W_BIG]) + bg_ref[...]
    gt_ref[0] = _sigmoid(mg).astype(_bf16)

    for g, ((_, dil), out_ref, tab_ref) in enumerate(zip(DIL_GROUPS, (d1_ref, d4_ref, d16_ref),
                                                         (t1_ref, t4_ref, t16_ref))):
        per = rows // dil
        if dil == 1:
            hg = hb
        else:
            hg = jnp.concatenate(
                [jnp.concatenate([hs_ref[c, pl.ds(r, per, stride=dil), :] for r in range(dil)], axis=0)
                 for c in range(d_model // LANES)], axis=1).astype(_bf16)
        cos_d, sin_d = tab_ref[0], tab_ref[1]
        for kind in range(3):
            off = OFF_DIL + (3 * g + kind) * DIL_WIDTH
            u = _dot(hg, w_ref[:, off:off + DIL_WIDTH])
            if kind == 2:
                ub = u.astype(_bf16)
            else:
                post = DIL_HEAD_DIM ** -0.5 * math.log2(math.e) if kind == 0 else 1.0
                ub = jnp.concatenate(
                    [(_rope_pairs(u[:, p * LANES:(p + 1) * LANES], cos_d, sin_d) * post).astype(_bf16)
                     for p in range(DIL_SLABS)], axis=1)
            for r in range(dil):
                out_ref[kind, 0, r] = ub[r * per:(r + 1) * per]


def _in_call(x, scale, shift, lw, tab_m, tabs_d):
    b, s, d = x.shape
    tm = min(TOKEN_TILE, s)
    nt = s // tm
    const = lambda *shape: pl.BlockSpec(shape, lambda bi, ti: (0,) * len(shape),
                                        pipeline_mode=pl.Buffered(1))
    tab = pl.BlockSpec((2, tm, LANES), lambda bi, ti: (0, ti, 0))
    in_specs = [
        pl.BlockSpec((1, tm, d), lambda bi, ti: (bi, ti, 0)),
        pl.BlockSpec((1, 1, d), lambda bi, ti: (bi, 0, 0)),
        pl.BlockSpec((1, 1, d), lambda bi, ti: (bi, 0, 0)),
        const(1, d),
        const(d, W_BIG),
        const(1, Q_LORA),
        const(MLA_HEADS * HEAD_PAD, Q_LORA),
        const(1, KV_LORA),
        const(KV_LORA, MLA_HEADS * HEAD_PAD),
        const(MLA_WIDTH, KV_LORA),
        const(1, 2 * d),
        tab, pl.BlockSpec((2, LANES, tm), lambda bi, ti: (0, 0, ti)), tab, tab, tab,
    ]
    dil_shapes = [jax.ShapeDtypeStruct((3, b, dl, s // dl, DIL_WIDTH), _bf16) for _, dl in DIL_GROUPS]
    dil_specs = [pl.BlockSpec((3, 1, dl, tm // dl, DIL_WIDTH), lambda bi, ti: (0, bi, 0, ti, 0))
                 for _, dl in DIL_GROUPS]
    out_shape = [
        jax.ShapeDtypeStruct((b, MLA_HEADS, nt, HEAD_PAD, tm), _bf16),
        jax.ShapeDtypeStruct((b, MLA_HEADS, s, HEAD_PAD), _bf16),
        jax.ShapeDtypeStruct((b, MLA_HEADS, nt, MLA_V, tm), _bf16),
        jax.ShapeDtypeStruct((b, s, MLA_WIDTH), _bf16),
        *dil_shapes,
        jax.ShapeDtypeStruct((b, s, DIL_WIDTH), _bf16),
        jax.ShapeDtypeStruct((b, s, 2 * d), _bf16),
    ]
    out_specs = [
        pl.BlockSpec((1, MLA_HEADS, 1, HEAD_PAD, tm), lambda bi, ti: (bi, 0, ti, 0, 0)),
        pl.BlockSpec((1, MLA_HEADS, tm, HEAD_PAD), lambda bi, ti: (bi, 0, ti, 0)),
        pl.BlockSpec((1, MLA_HEADS, 1, MLA_V, tm), lambda bi, ti: (bi, 0, ti, 0, 0)),
        pl.BlockSpec((1, tm, MLA_WIDTH), lambda bi, ti: (bi, ti, 0)),
        *dil_specs,
        pl.BlockSpec((1, tm, DIL_WIDTH), lambda bi, ti: (bi, ti, 0)),
        pl.BlockSpec((1, tm, 2 * d), lambda bi, ti: (bi, ti, 0)),
    ]
    kern = functools.partial(_in_kernel, q_scale=(MLA_NOPE + MLA_ROPE) ** -0.5 * math.log2(math.e))
    return pl.pallas_call(
        kern,
        grid=(b, nt),
        in_specs=in_specs,
        out_specs=out_specs,
        out_shape=out_shape,
        scratch_shapes=[pltpu.VMEM((d // LANES, tm, LANES), _f32)],
        compiler_params=pltpu.CompilerParams(vmem_limit_bytes=VMEM_LIMIT),
        name="in_proj",
    )(x, scale, shift, lw["g_norm"], lw["w_big"], lw["g_cq"], lw["w_uq"], lw["g_ckv"],
      lw["w_k"], lw["w_vt"], lw["b_gate"], tab_m, jnp.swapaxes(tab_m, 1, 2), *tabs_d)


def _mla_kernel(q_ref, k_ref, vt_ref, o_ref, sa_ref, sb_ref, m_ref, acc_ref, *, n_tiles, tq,
                n_chunks, tk, unroll):
    ones = jnp.ones((BF16_ROWS, tk), _bf16)
    per = tk // vt_ref.shape[-1]
    per_q = tq // q_ref.shape[-1]
    bufs = (sa_ref, sb_ref)

    def scores(qi, c, dst_ref):
        qt = jnp.concatenate([q_ref[0, 0, qi * per_q + j] for j in range(per_q)], axis=1)
        kc = k_ref[0, 0, pl.ds(pl.multiple_of(c * tk, tk), tk), :]
        st = _dot(kc, qt)
        dst_ref[...] = st
        return jnp.max(st, axis=0, keepdims=True)

    def consume(c, src_ref, cmax):
        m_old = m_ref[...]
        m_new = jnp.maximum(m_old, cmax)
        alpha = jnp.exp2(m_old - m_new)
        p = jnp.exp2(src_ref[...] - m_new).astype(_bf16)
        vc = jnp.concatenate([vt_ref[0, 0, c * per + j] for j in range(per)], axis=1)
        va = jnp.concatenate([vc, ones], axis=0)
        acc_ref[...] = alpha * acc_ref[...] + _dot(va, p)
        m_ref[...] = m_new

    def steps(qi, first, count, cm):
        for u in range(count):
            cm_next = scores(qi, first + u + 1, bufs[(u + 1) % 2])
            consume(first + u, bufs[u % 2], cm)
            cm = cm_next
        return cm

    def tile(qi, cm):
        m_ref[...] = jnp.full(m_ref.shape, NEG_INF, _f32)
        acc_ref[...] = jnp.zeros(acc_ref.shape, _f32)
        cm = lax.fori_loop(0, n_chunks // unroll - 1,
                           lambda j, cm: steps(qi, j * unroll, unroll, cm), cm)
        cm = steps(qi, n_chunks - unroll, unroll - 1, cm)
        cm_next = scores(jnp.minimum(qi + 1, n_tiles - 1), 0, bufs[0])
        consume(n_chunks - 1, bufs[1], cm)
        acc = acc_ref[...]
        o_ref[0, 0, qi] = (acc[:MLA_V] / acc[MLA_V:MLA_V + 1]).astype(_bf16)
        return cm_next

    lax.fori_loop(0, n_tiles, tile, scores(0, 0, sa_ref))


def _mla_tiles(s):
    if s // MLA_K_CHUNK >= 4 * MLA_UNROLL:
        return MLA_Q_TILE, MLA_K_CHUNK, MLA_UNROLL
    return min(2 * MLA_Q_TILE, s), min(MLA_K_CHUNK // 2, s // 2), 2


def _mla_call(q, k, vt):
    b, hh, s, _ = k.shape
    nv, tv = vt.shape[2], vt.shape[4]
    tq, tk, unroll = _mla_tiles(s)
    n_chunks, n_tiles = s // tk, s // tq
    assert n_chunks % unroll == 0 and unroll % 2 == 0 and tk % tv == 0 and tq % tv == 0
    kern = functools.partial(_mla_kernel, n_tiles=n_tiles, tq=tq, n_chunks=n_chunks, tk=tk,
                             unroll=unroll)
    return pl.pallas_call(
        kern,
        grid=(b, hh),
        in_specs=[
            pl.BlockSpec((1, 1, nv, HEAD_PAD, tv), lambda bi, hi: (bi, hi, 0, 0, 0)),
            pl.BlockSpec((1, 1, s, HEAD_PAD), lambda bi, hi: (bi, hi, 0, 0)),
            pl.BlockSpec((1, 1, nv, MLA_V, tv), lambda bi, hi: (bi, hi, 0, 0, 0)),
        ],
        out_specs=pl.BlockSpec((1, 1, n_tiles, MLA_V, tq), lambda bi, hi: (bi, hi, 0, 0, 0)),
        out_shape=jax.ShapeDtypeStruct((b, hh, n_tiles, MLA_V, tq), _bf16),
        scratch_shapes=[pltpu.VMEM((tk, tq), _f32), pltpu.VMEM((tk, tq), _f32),
                        pltpu.VMEM((1, tq), _f32), pltpu.VMEM((MLA_V + BF16_ROWS, tq), _f32)],
        compiler_params=pltpu.CompilerParams(vmem_limit_bytes=VMEM_LIMIT),
        name="mla_attn",
    )(q, k, vt)


def _band_kernel(q_ref, kp_ref, kc_ref, kn_ref, vp_ref, vc_ref, vn_ref, o_ref, stat_ref,
                 *, dil, tq, seq):
    i = pl.program_id(1)
    nk = BAND_SUB + 2 * BAND_HALF
    row = lax.broadcasted_iota(jnp.int32, (BAND_SUB, nk), 0)
    col = lax.broadcasted_iota(jnp.int32, (BAND_SUB, nk), 1)
    band = jnp.abs(col - BAND_HALF - row) <= BAND_HALF
    lane = lax.broadcasted_iota(jnp.int32, (1, LANES), 1)
    first_head = (lane % HALF_LANES) < DIL_HEAD_DIM // 2
    low_half = lane < DIL_HEAD_DIM

    def residue(r, carry):
        q = q_ref[0, 0, r]
        kk = jnp.concatenate([kp_ref[0, 0, r], kc_ref[0, 0, r], kn_ref[0, 0, r]], axis=0)
        vv = jnp.concatenate([vp_ref[0, 0, r], vc_ref[0, 0, r], vn_ref[0, 0, r]], axis=0)
        for sb in range(tq // BAND_SUB):
            a = sb * BAND_SUB
            if 0 < sb < tq // BAND_SUB - 1:
                valid = band
            else:
                kidx = i * tq + (a - BAND_HALF) + col
                valid = band & (kidx >= 0) & (kidx < seq)
            dst = pl.ds(r + a * dil, BAND_SUB, stride=dil) if dil > 1 else pl.ds(a, BAND_SUB)
            stats = [None, None]
            for p in range(DIL_SLABS):
                sl = slice(p * LANES, (p + 1) * LANES)
                qp = q[a:a + BAND_SUB, sl]
                kp = kk[a:a + nk, sl]
                vp = vv[a:a + nk, sl]
                res = []
                for sel in (first_head, jnp.logical_not(first_head)):
                    qm = jnp.where(sel, qp, jnp.zeros_like(qp))
                    sc = jnp.where(valid, _dot_nt(qm, kp), NEG_INF)
                    m = jnp.max(sc, axis=-1, keepdims=True)
                    e = jnp.exp2(sc - m)
                    l = jnp.sum(e, axis=-1, keepdims=True)
                    res.append((_dot(e.astype(_bf16), vp), m, l))
                o_ref[0, p, dst, :] = jnp.where(low_half, res[0][0], res[1][0])
                even = lane % (2 * LSE_LANES) < LSE_LANES
                here = lane // (2 * LSE_LANES) == p
                for j in range(2):
                    pair = jnp.where(even, res[0][1 + j], res[1][1 + j])
                    stats[j] = pair if p == 0 else jnp.where(here, pair, stats[j])
            stat_ref[0, 0, dst, :] = stats[0]
            stat_ref[0, 1, dst, :] = stats[1]
        return carry

    lax.fori_loop(0, dil, residue, 0, unroll=min(dil, BAND_UNROLL))


def _band_call(qkv, dil):
    _, b, _, seq, w = qkv.shape
    s = seq * dil
    tq = max(BAND_SUB, BAND_ROWS // dil)
    assert tq % BAND_SUB == 0 and seq % tq == 0
    hb = tq // BAND_HALF
    last = seq // BAND_HALF - 1

    def cur(j):
        return pl.BlockSpec((1, 1, dil, tq, w), lambda bi, i: (j, bi, 0, i, 0))

    def prev(j):
        return pl.BlockSpec((1, 1, dil, BAND_HALF, w),
                            lambda bi, i: (j, bi, 0, jnp.maximum(i * hb - 1, 0), 0))

    def nxt(j):
        return pl.BlockSpec((1, 1, dil, BAND_HALF, w),
                            lambda bi, i: (j, bi, 0, jnp.minimum((i + 1) * hb, last), 0))

    kern = functools.partial(_band_kernel, dil=dil, tq=tq, seq=seq)
    return pl.pallas_call(
        kern,
        grid=(b, seq // tq),
        in_specs=[cur(0), prev(1), cur(1), nxt(1), prev(2), cur(2), nxt(2)],
        out_specs=[pl.BlockSpec((1, DIL_SLABS, dil * tq, LANES), lambda bi, i: (bi, 0, i, 0)),
                   pl.BlockSpec((1, 2, dil * tq, LANES), lambda bi, i: (bi, 0, i, 0))],
        out_shape=[jax.ShapeDtypeStruct((b, DIL_SLABS, s, LANES), _f32),
                   jax.ShapeDtypeStruct((b, 2, s, LANES), _f32)],
        compiler_params=pltpu.CompilerParams(vmem_limit_bytes=VMEM_LIMIT),
        name=f"band_attn_d{dil}",
    )(qkv, qkv, qkv, qkv, qkv, qkv, qkv)


def _out_kernel(x_ref, ot_ref, zm_ref, o0_ref, o1_ref, o2_ref, l0_ref, l1_ref, l2_ref,
                zd_ref, gt_ref, gate_ref, wpa_ref, wpb_ref, wo_ref, gf_ref, ex_ref, y_ref,
                *, final_norm):
    d = x_ref.shape[-1]
    ot = jnp.concatenate([ot_ref[0, hd, 0] for hd in range(MLA_HEADS)], axis=0)
    o_mla = jnp.transpose(ot.astype(_f32)) * zm_ref[0].astype(_f32)
    a = _dot(o_mla.astype(_bf16), wpa_ref[...])

    m0, m1, m2 = l0_ref[0, 0], l1_ref[0, 0], l2_ref[0, 0]
    mx = jnp.maximum(jnp.maximum(m0, m1), m2)
    w0, w1, w2 = jnp.exp2(m0 - mx), jnp.exp2(m1 - mx), jnp.exp2(m2 - mx)
    inv = 1.0 / (w0 * l0_ref[0, 1] + w1 * l1_ref[0, 1] + w2 * l2_ref[0, 1])
    o_dil = None
    for w, o_ref in ((w0, o0_ref), (w1, o1_ref), (w2, o2_ref)):
        wide = _dot((w * inv).astype(_bf16), ex_ref[...])
        term = wide * jnp.concatenate([o_ref[0, p] for p in range(DIL_SLABS)], axis=1)
        o_dil = term if o_dil is None else o_dil + term
    o_dil = o_dil * zd_ref[0].astype(_f32)
    bb = _dot(o_dil.astype(_bf16), wpb_ref[...])

    gt = gt_ref[0].astype(_f32)
    u = gt[:, :d] * a + gt[:, d:] * bb
    y = x_ref[0] + gate_ref[0] * _dot(u.astype(_bf16), wo_ref[...])
    if final_norm:
        y = _rms(y, gf_ref[...])
    y_ref[0] = y


def _out_call(x, ot, zm, outs, lses, zd, gt, gate, lw, g_final, final_norm):
    b, s, d = x.shape
    tm = min(TOKEN_TILE, s)
    tok = lambda w: pl.BlockSpec((1, tm, w), lambda bi, ti: (bi, ti, 0))
    slab = pl.BlockSpec((1, DIL_SLABS, tm, LANES), lambda bi, ti: (bi, 0, ti, 0))
    stat = pl.BlockSpec((1, 2, tm, LANES), lambda bi, ti: (bi, 0, ti, 0))
    const = lambda *shape: pl.BlockSpec(shape, lambda bi, ti: (0,) * len(shape))
    kern = functools.partial(_out_kernel, final_norm=final_norm)
    assert ot.shape[-1] % tm == 0
    per = ot.shape[-1] // tm
    spread = np.zeros((LANES, DIL_WIDTH), np.float32)
    for hd in range(DIL_HEADS):
        spread[hd * LSE_LANES, hd * DIL_HEAD_DIM:(hd + 1) * DIL_HEAD_DIM] = 1.0
    return pl.pallas_call(
        kern,
        grid=(b, s // tm),
        in_specs=[tok(d),
                  pl.BlockSpec((1, MLA_HEADS, 1, MLA_V, tm),
                               lambda bi, ti: (bi, 0, ti // per, 0, ti % per)),
                  tok(MLA_WIDTH),
                  slab, slab, slab, stat, stat, stat,
                  tok(DIL_WIDTH), tok(2 * d),
                  pl.BlockSpec((1, 1, d), lambda bi, ti: (bi, 0, 0)),
                  const(MLA_WIDTH, d), const(DIL_WIDTH, d), const(d, d), const(1, d),
                  const(LANES, DIL_WIDTH)],
        out_specs=tok(d),
        out_shape=jax.ShapeDtypeStruct((b, s, d), _f32),
        compiler_params=pltpu.CompilerParams(vmem_limit_bytes=VMEM_LIMIT),
        name="out_proj",
    )(x, ot, zm, outs[0], outs[1], outs[2], lses[0], lses[1], lses[2],
      zd, gt, gate, lw["w_pa"], lw["w_pb"], lw["w_out"], g_final, jnp.asarray(spread, _bf16))


def _mla_lane_order():
    half, mid = MLA_ROPE // 2, HALF_LANES
    first = mid - half
    src = np.full((HEAD_PAD,), -1, np.int64)
    src[0:half] = MLA_NOPE + np.arange(half)
    src[half:mid] = np.arange(first)
    src[mid:mid + half] = MLA_NOPE + half + np.arange(half)
    src[mid + half:mid + half + MLA_NOPE - first] = np.arange(first, MLA_NOPE)
    return src


def _gather_cols(w, src):
    picked = jnp.take(w, jnp.asarray(np.maximum(src, 0)), axis=1)
    return jnp.where(jnp.asarray(src >= 0)[None, :], picked, 0.0)


def _dil_perm():
    idx = []
    half = DIL_HEAD_DIM // 2
    for p in range(DIL_HEADS // 2):
        ha, hb = 2 * p * DIL_HEAD_DIM, (2 * p + 1) * DIL_HEAD_DIM
        idx += list(range(ha, ha + half)) + list(range(hb, hb + half))
        idx += list(range(ha + half, ha + 2 * half)) + list(range(hb + half, hb + 2 * half))
    return np.asarray(idx, np.int64)


def _pack_layer(w_in, g_norm, b_gate, g_cq, w_uq, g_ckv, w_ukv, w_pa, w_pb, w_out):
    d = w_in.shape[0]
    splits = np.cumsum([Q_LORA, KV_LORA, MLA_ROPE, MLA_WIDTH] + [DIL_WIDTH] * (3 * N_GROUPS)
                       + [DIL_WIDTH])
    parts = jnp.split(w_in, list(splits), axis=1)
    perm = _dil_perm()
    lane_src = _mla_lane_order()
    kr_src = np.where(lane_src >= MLA_NOPE, lane_src - MLA_NOPE, -1)
    cols = [parts[0], parts[1], _gather_cols(parts[2], kr_src), parts[3]]
    for j in range(3 * N_GROUPS):
        pj = parts[4 + j]
        cols.append(pj if j % 3 == 2 else jnp.take(pj, jnp.asarray(perm), axis=1))
    cols += [parts[4 + 3 * N_GROUPS], parts[5 + 3 * N_GROUPS]]
    w_big = jnp.concatenate(cols, axis=1).astype(_bf16)

    dq = MLA_NOPE + MLA_ROPE
    q_src = np.concatenate([np.where(lane_src >= 0, h * dq + lane_src, -1) for h in range(MLA_HEADS)])
    w_uq_p = jnp.transpose(_gather_cols(w_uq, q_src)).astype(_bf16)
    dkv = MLA_NOPE + MLA_V
    nope_src = np.where((lane_src >= 0) & (lane_src < MLA_NOPE), lane_src, -1)
    k_src = np.concatenate([np.where(nope_src >= 0, h * dkv + nope_src, -1) for h in range(MLA_HEADS)])
    w_k = _gather_cols(w_ukv, k_src).astype(_bf16)
    v_src = np.concatenate([h * dkv + MLA_NOPE + np.arange(MLA_V) for h in range(MLA_HEADS)])
    w_vt = jnp.transpose(jnp.take(w_ukv, jnp.asarray(v_src), axis=1)).astype(_bf16)

    return dict(
        w_big=w_big, g_norm=g_norm.reshape(1, d), b_gate=b_gate.reshape(1, -1),
        g_cq=g_cq.reshape(1, -1), w_uq=w_uq_p, g_ckv=g_ckv.reshape(1, -1), w_k=w_k,
        w_vt=w_vt,
        w_pa=w_pa.astype(_bf16), w_pb=w_pb.astype(_bf16), w_out=w_out.astype(_bf16))


def _rope_tables(s, tm):
    pos = np.arange(s, dtype=np.float64)[:, None]
    lane = np.arange(LANES)

    def table(freq_idx, active, n_freq):
        inv = np.power(ROPE_THETA, -2.0 * freq_idx / (2 * n_freq))
        ang = pos * inv[None, :]
        cos = np.where(active[None, :], np.cos(ang), 1.0)
        sign = np.where(lane < HALF_LANES, -1.0, 1.0)
        sin = np.where(active[None, :], np.sin(ang) * sign[None, :], 0.0)
        return np.stack([cos, sin]).astype(np.float32)

    half = MLA_ROPE // 2
    in_half = lane % HALF_LANES
    tab_m = table(in_half.clip(0, half - 1).astype(np.float64), in_half < half, half)
    n_dil = DIL_HEAD_DIM // 2
    tab_d = jnp.asarray(table((lane % n_dil).astype(np.float64), np.ones(LANES, bool), n_dil))
    tabs = []
    for _, dil in DIL_GROUPS:
        t = tab_d.reshape(2, s // tm, tm // dil, dil, LANES)
        tabs.append(jnp.swapaxes(t, 2, 3).reshape(2, s, LANES))
    return jnp.asarray(tab_m), tabs


def _trunk(x, mods, layers, g_final):
    b, s, d = x.shape
    assert s % (DIL_GROUPS[-1][1] * BAND_SUB) == 0 and s % TOKEN_TILE == 0
    tab_m, tabs_d = _rope_tables(s, min(TOKEN_TILE, s))
    depth = len(layers)
    for l, lw in enumerate(layers):
        mod = mods[l]
        shift, scale, gate = [mod[:, None, j * d:(j + 1) * d] for j in range(3)]
        q, k, vt, zm, d1, d4, d16, zd, gt = _in_call(x, scale, shift, lw, tab_m, tabs_d)
        ot = _mla_call(q, k, vt)
        outs, lses = [], []
        for qkv, (_, dl) in zip((d1, d4, d16), DIL_GROUPS):
            o_g, lse_g = _band_call(qkv, dl)
            outs.append(o_g)
            lses.append(lse_g)
        x = _out_call(x, ot, zm, outs, lses, zd, gt, gate, lw, g_final.reshape(1, d),
                      final_norm=(l == depth - 1))
    return x


def kernel(x_prompt, x_sample, c_prompt, c_sample, w_ada, b_ada, g_norm, w_in, b_gate, g_cq, w_uq,
           g_ckv, w_ukv, w_pa, w_pb, w_out, g_final):
    depth = w_in.shape[0]
    bp, bs = c_prompt.shape[0], c_sample.shape[0]
    rows = -(-(bp + bs) // 8) * 8
    c_all = jnp.concatenate([c_prompt, c_sample], axis=0)
    c_all = jnp.pad(c_all, ((0, rows - bp - bs), (0, 0)))
    mods = _ada_call(c_all, w_ada, b_ada)
    layers = [_pack_layer(w_in[l], g_norm[l], b_gate[l], g_cq[l], w_uq[l], g_ckv[l], w_ukv[l],
                          w_pa[l], w_pb[l], w_out[l]) for l in range(depth)]
    y_prompt = _trunk(x_prompt, mods[:, :bp], layers, g_final)
    y_sample = _trunk(x_sample, mods[:, bp:bp + bs], layers, g_final)
    return (y_prompt, y_sample)
```

```python
import functools
import math

import numpy as np
import jax
import jax.numpy as jnp
from jax import lax
from jax.experimental import pallas as pl
from jax.experimental.pallas import tpu as pltpu

ROPE_THETA = 10000.0
EPS = 1e-6
NEG_INF = -1e30

MLA_HEADS = 8
MLA_NOPE = 64
MLA_ROPE = 32
MLA_V = 64
Q_LORA = 384
KV_LORA = 256
MLA_WIDTH = MLA_HEADS * MLA_V
DIL_GROUPS = ((128, 1), (512, 4), (2048, 16))
DIL_HEADS = 8
DIL_HEAD_DIM = 64
DIL_WIDTH = DIL_HEADS * DIL_HEAD_DIM
N_GROUPS = len(DIL_GROUPS)
BAND_HALF = 64

LANES = 128
BF16_ROWS = 16
HALF_LANES = LANES // 2
KR_PAD = LANES
HEAD_PAD = LANES
DIL_SLABS = DIL_WIDTH // LANES
LSE_LANES = LANES // DIL_HEADS

OFF_CQ = 0
OFF_CKV = OFF_CQ + Q_LORA
OFF_KR = OFF_CKV + KV_LORA
OFF_ZM = OFF_KR + KR_PAD
OFF_DIL = OFF_ZM + MLA_WIDTH
OFF_ZD = OFF_DIL + 3 * N_GROUPS * DIL_WIDTH
OFF_MG = OFF_ZD + DIL_WIDTH
W_BIG = OFF_MG + 2 * 1024

TOKEN_TILE = 512
MLA_Q_TILE = 512
MLA_K_CHUNK = 1024
MLA_UNROLL = 4
BAND_SUB = 128
BAND_ROWS = 2048
BAND_UNROLL = 16
VMEM_LIMIT = 56 * 1024 * 1024

_f32 = jnp.float32
_bf16 = jnp.bfloat16
_NT = (((1,), (1,)), ((), ()))


def _dot(a, b):
    return jnp.dot(a, b, preferred_element_type=_f32)


def _dot_nt(a, b):
    return lax.dot_general(a, b, _NT, preferred_element_type=_f32)


def _rms(x, g):
    return x * lax.rsqrt(jnp.mean(x * x, axis=-1, keepdims=True) + EPS) * g


def _sigmoid(x):
    return 1.0 / (1.0 + jnp.exp(-x))


def _rope_pairs(x, cos, sin):
    return x * cos + pltpu.roll(x, HALF_LANES, axis=1) * sin


def _ada_kernel(c_ref, w_ref, b_ref, o_ref):
    c = c_ref[...]
    a = c * _sigmoid(c)
    a_hi = a.astype(_bf16)
    a_lo = (a - a_hi.astype(_f32)).astype(_bf16)
    w = w_ref[0]
    w_hi = w.astype(_bf16)
    w_lo = (w - w_hi.astype(_f32)).astype(_bf16)
    o_ref[0] = _dot(a_hi, w_hi) + _dot(a_hi, w_lo) + _dot(a_lo, w_hi) + b_ref[0]


def _ada_call(c_all, w_ada, b_ada):
    depth, d, n = w_ada.shape
    rows = c_all.shape[0]
    nb = n // d
    return pl.pallas_call(
        _ada_kernel,
        grid=(depth, nb),
        in_specs=[
            pl.BlockSpec((rows, d), lambda l, j: (0, 0)),
            pl.BlockSpec((1, d, d), lambda l, j: (l, 0, j)),
            pl.BlockSpec((1, 1, d), lambda l, j: (l, 0, j)),
        ],
        out_specs=pl.BlockSpec((1, rows, d), lambda l, j: (l, 0, j)),
        out_shape=jax.ShapeDtypeStruct((depth, rows, n), _f32),
        compiler_params=pltpu.CompilerParams(vmem_limit_bytes=VMEM_LIMIT),
        name="adaln",
    )(c_all, w_ada, b_ada.reshape(depth, 1, n))


def _in_kernel(x_ref, sc_ref, sh_ref, gn_ref, w_ref, gcq_ref, wuq_ref, gckv_ref, wk_ref,
               wvt_ref, bg_ref, tm_ref, tmt_ref, t1_ref, t4_ref, t16_ref,
               q_ref, k_ref, vt_ref, zm_ref, d1_ref, d4_ref, d16_ref, zd_ref, gt_ref,
               hs_ref, *, q_scale):
    x = x_ref[0]
    rows, d_model = x.shape
    h = _rms(x, gn_ref[...]) * (1.0 + sc_ref[0]) + sh_ref[0]
    hb = h.astype(_bf16)
    for c in range(d_model // LANES):
        hs_ref[c] = h[:, c * LANES:(c + 1) * LANES]
    cos_m, sin_m = tm_ref[0], tm_ref[1]

    cq = _dot(hb, w_ref[:, OFF_CQ:OFF_CQ + Q_LORA])
    cqn = _rms(cq, gcq_ref[...]).astype(_bf16)
    qt = _dot_nt(wuq_ref[...], cqn)
    cos_t, sin_t = tmt_ref[0], tmt_ref[1]
    for hd in range(MLA_HEADS):
        xh = qt[hd * HEAD_PAD:(hd + 1) * HEAD_PAD, :]
        partner = jnp.concatenate([xh[HEAD_PAD // 2:], xh[:HEAD_PAD // 2]], axis=0)
        q_ref[0, hd, 0] = ((xh * cos_t + partner * sin_t) * q_scale).astype(_bf16)

    ckv = _dot(hb, w_ref[:, OFF_CKV:OFF_CKV + KV_LORA])
    ckvn = _rms(ckv, gckv_ref[...]).astype(_bf16)
    kr = _dot(hb, w_ref[:, OFF_KR:OFF_KR + KR_PAD])
    k = _dot(ckvn, wk_ref[...])
    for hd in range(MLA_HEADS):
        sl = slice(hd * HEAD_PAD, (hd + 1) * HEAD_PAD)
        k_ref[0, hd] = _rope_pairs(k[:, sl] + kr, cos_m, sin_m).astype(_bf16)
    vt = _dot_nt(wvt_ref[...], ckvn)
    for hd in range(MLA_HEADS):
        vt_ref[0, hd, 0] = vt[hd * MLA_V:(hd + 1) * MLA_V, :].astype(_bf16)

    zm = _dot(hb, w_ref[:, OFF_ZM:OFF_ZM + MLA_WIDTH])
    zm_ref[0] = (zm * _sigmoid(zm)).astype(_bf16)

    zd = _dot(hb, w_ref[:, OFF_ZD:OFF_ZD + DIL_WIDTH])
    zd_ref[0] = (zd * _sigmoid(zd)).astype(_bf16)

    mg = _dot(hb, w_ref[:, OFF_MG:W_BIG]) + bg_ref[...]
    gt_ref[0] = _sigmoid(mg).astype(_bf16)

    for g, ((_, dil), out_ref, tab_ref) in enumerate(zip(DIL_GROUPS, (d1_ref, d4_ref, d16_ref),
                                                         (t1_ref, t4_ref, t16_ref))):
        per = rows // dil
        if dil == 1:
            hg = hb
        else:
            hg = jnp.concatenate(
                [jnp.concatenate([hs_ref[c, pl.ds(r, per, stride=dil), :] for r in range(dil)], axis=0)
                 for c in range(d_model // LANES)], axis=1).astype(_bf16)
        cos_d, sin_d = tab_ref[0], tab_ref[1]
        for kind in range(3):
            off = OFF_DIL + (3 * g + kind) * DIL_WIDTH
            u = _dot(hg, w_ref[:, off:off + DIL_WIDTH])
            if kind == 2:
                ub = u.astype(_bf16)
            else:
                post = DIL_HEAD_DIM ** -0.5 * math.log2(math.e) if kind == 0 else 1.0
                ub = jnp.concatenate(
                    [(_rope_pairs(u[:, p * LANES:(p + 1) * LANES], cos_d, sin_d) * post).astype(_bf16)
                     for p in range(DIL_SLABS)], axis=1)
            for r in range(dil):
                out_ref[kind, 0, r] = ub[r * per:(r + 1) * per]


def _in_call(x, scale, shift, lw, tab_m, tabs_d):
    b, s, d = x.shape
    tm = min(TOKEN_TILE, s)
    nt = s // tm
    const = lambda *shape: pl.BlockSpec(shape, lambda bi, ti: (0,) * len(shape),
                                        pipeline_mode=pl.Buffered(1))
    tab = pl.BlockSpec((2, tm, LANES), lambda bi, ti: (0, ti, 0))
    in_specs = [
        pl.BlockSpec((1, tm, d), lambda bi, ti: (bi, ti, 0)),
        pl.BlockSpec((1, 1, d), lambda bi, ti: (bi, 0, 0)),
        pl.BlockSpec((1, 1, d), lambda bi, ti: (bi, 0, 0)),
        const(1, d),
        const(d, W_BIG),
        const(1, Q_LORA),
        const(MLA_HEADS * HEAD_PAD, Q_LORA),
        const(1, KV_LORA),
        const(KV_LORA, MLA_HEADS * HEAD_PAD),
        const(MLA_WIDTH, KV_LORA),
        const(1, 2 * d),
        tab, pl.BlockSpec((2, LANES, tm), lambda bi, ti: (0, 0, ti)), tab, tab, tab,
    ]
    dil_shapes = [jax.ShapeDtypeStruct((3, b, dl, s // dl, DIL_WIDTH), _bf16) for _, dl in DIL_GROUPS]
    dil_specs = [pl.BlockSpec((3, 1, dl, tm // dl, DIL_WIDTH), lambda bi, ti: (0, bi, 0, ti, 0))
                 for _, dl in DIL_GROUPS]
    out_shape = [
        jax.ShapeDtypeStruct((b, MLA_HEADS, nt, HEAD_PAD, tm), _bf16),
        jax.ShapeDtypeStruct((b, MLA_HEADS, s, HEAD_PAD), _bf16),
        jax.ShapeDtypeStruct((b, MLA_HEADS, nt, MLA_V, tm), _bf16),
        jax.ShapeDtypeStruct((b, s, MLA_WIDTH), _bf16),
        *dil_shapes,
        jax.ShapeDtypeStruct((b, s, DIL_WIDTH), _bf16),
        jax.ShapeDtypeStruct((b, s, 2 * d), _bf16),
    ]
    out_specs = [
        pl.BlockSpec((1, MLA_HEADS, 1, HEAD_PAD, tm), lambda bi, ti: (bi, 0, ti, 0, 0)),
        pl.BlockSpec((1, MLA_HEADS, tm, HEAD_PAD), lambda bi, ti: (bi, 0, ti, 0)),
        pl.BlockSpec((1, MLA_HEADS, 1, MLA_V, tm), lambda bi, ti: (bi, 0, ti, 0, 0)),
        pl.BlockSpec((1, tm, MLA_WIDTH), lambda bi, ti: (bi, ti, 0)),
        *dil_specs,
        pl.BlockSpec((1, tm, DIL_WIDTH), lambda bi, ti: (bi, ti, 0)),
        pl.BlockSpec((1, tm, 2 * d), lambda bi, ti: (bi, ti, 0)),
    ]
    kern = functools.partial(_in_kernel, q_scale=(MLA_NOPE + MLA_ROPE) ** -0.5 * math.log2(math.e))
    return pl.pallas_call(
        kern,
        grid=(b, nt),
        in_specs=in_specs,
        out_specs=out_specs,
        out_shape=out_shape,
        scratch_shapes=[pltpu.VMEM((d // LANES, tm, LANES), _f32)],
        compiler_params=pltpu.CompilerParams(vmem_limit_bytes=VMEM_LIMIT),
        name="in_proj",
    )(x, scale, shift, lw["g_norm"], lw["w_big"], lw["g_cq"], lw["w_uq"], lw["g_ckv"],
      lw["w_k"], lw["w_vt"], lw["b_gate"], tab_m, jnp.swapaxes(tab_m, 1, 2), *tabs_d)


def _mla_kernel(q_ref, k_ref, vt_ref, o_ref, sa_ref, sb_ref, m_ref, acc_ref, *, n_tiles, tq,
                n_chunks, tk, unroll):
    ones = jnp.ones((BF16_ROWS, tk), _bf16)
    per = tk // vt_ref.shape[-1]
    per_q = tq // q_ref.shape[-1]
    bufs = (sa_ref, sb_ref)

    def scores(qi, c, dst_ref):
        qt = jnp.concatenate([q_ref[0, 0, qi * per_q + j] for j in range(per_q)], axis=1)
        kc = k_ref[0, 0, pl.ds(pl.multiple_of(c * tk, tk), tk), :]
        st = _dot(kc, qt)
        dst_ref[...] = st
        return jnp.max(st, axis=0, keepdims=True)

    def consume(c, src_ref, cmax):
        m_old = m_ref[...]
        m_new = jnp.maximum(m_old, cmax)
        alpha = jnp.exp2(m_old - m_new)
        p = jnp.exp2(src_ref[...] - m_new).astype(_bf16)
        vc = jnp.concatenate([vt_ref[0, 0, c * per + j] for j in range(per)], axis=1)
        va = jnp.concatenate([vc, ones], axis=0)
        acc_ref[...] = alpha * acc_ref[...] + _dot(va, p)
        m_ref[...] = m_new

    def steps(qi, first, count, cm):
        for u in range(count):
            cm_next = scores(qi, first + u + 1, bufs[(u + 1) % 2])
            consume(first + u, bufs[u % 2], cm)
            cm = cm_next
        return cm

    def tile(qi, cm):
        m_ref[...] = jnp.full(m_ref.shape, NEG_INF, _f32)
        acc_ref[...] = jnp.zeros(acc_ref.shape, _f32)
        cm = lax.fori_loop(0, n_chunks // unroll - 1,
                           lambda j, cm: steps(qi, j * unroll, unroll, cm), cm)
        cm = steps(qi, n_chunks - unroll, unroll - 1, cm)
        cm_next = scores(jnp.minimum(qi + 1, n_tiles - 1), 0, bufs[0])
        consume(n_chunks - 1, bufs[1], cm)
        acc = acc_ref[...]
        o_ref[0, 0, qi] = (acc[:MLA_V] / acc[MLA_V:MLA_V + 1]).astype(_bf16)
        return cm_next

    lax.fori_loop(0, n_tiles, tile, scores(0, 0, sa_ref))


def _mla_tiles(s):
    if s // MLA_K_CHUNK >= 4 * MLA_UNROLL:
        return MLA_Q_TILE, MLA_K_CHUNK, MLA_UNROLL
    return min(2 * MLA_Q_TILE, s), min(MLA_K_CHUNK // 2, s // 2), 2


def _mla_call(q, k, vt):
    b, hh, s, _ = k.shape
    nv, tv = vt.shape[2], vt.shape[4]
    tq, tk, unroll = _mla_tiles(s)
    n_chunks, n_tiles = s // tk, s // tq
    assert n_chunks % unroll == 0 and unroll % 2 == 0 and tk % tv == 0 and tq % tv == 0
    kern = functools.partial(_mla_kernel, n_tiles=n_tiles, tq=tq, n_chunks=n_chunks, tk=tk,
                             unroll=unroll)
    return pl.pallas_call(
        kern,
        grid=(b, hh),
        in_specs=[
            pl.BlockSpec((1, 1, nv, HEAD_PAD, tv), lambda bi, hi: (bi, hi, 0, 0, 0)),
            pl.BlockSpec((1, 1, s, HEAD_PAD), lambda bi, hi: (bi, hi, 0, 0)),
            pl.BlockSpec((1, 1, nv, MLA_V, tv), lambda bi, hi: (bi, hi, 0, 0, 0)),
        ],
        out_specs=pl.BlockSpec((1, 1, n_tiles, MLA_V, tq), lambda bi, hi: (bi, hi, 0, 0, 0)),
        out_shape=jax.ShapeDtypeStruct((b, hh, n_tiles, MLA_V, tq), _bf16),
        scratch_shapes=[pltpu.VMEM((tk, tq), _f32), pltpu.VMEM((tk, tq), _f32),
                        pltpu.VMEM((1, tq), _f32), pltpu.VMEM((MLA_V + BF16_ROWS, tq), _f32)],
        compiler_params=pltpu.CompilerParams(vmem_limit_bytes=VMEM_LIMIT),
        name="mla_attn",
    )(q, k, vt)


def _band_kernel(q_ref, kp_ref, kc_ref, kn_ref, vp_ref, vc_ref, vn_ref, o_ref, stat_ref,
                 *, dil, tq, seq):
    i = pl.program_id(1)
    nk = BAND_SUB + 2 * BAND_HALF
    row = lax.broadcasted_iota(jnp.int32, (BAND_SUB, nk), 0)
    col = lax.broadcasted_iota(jnp.int32, (BAND_SUB, nk), 1)
    band = jnp.abs(col - BAND_HALF - row) <= BAND_HALF
    lane = lax.broadcasted_iota(jnp.int32, (1, LANES), 1)
    first_head = (lane % HALF_LANES) < DIL_HEAD_DIM // 2
    low_half = lane < DIL_HEAD_DIM

    def residue(r, carry):
        q = q_ref[0, 0, r]
        kk = jnp.concatenate([kp_ref[0, 0, r], kc_ref[0, 0, r], kn_ref[0, 0, r]], axis=0)
        vv = jnp.concatenate([vp_ref[0, 0, r], vc_ref[0, 0, r], vn_ref[0, 0, r]], axis=0)
        for sb in range(tq // BAND_SUB):
            a = sb * BAND_SUB
            if 0 < sb < tq // BAND_SUB - 1:
                valid = band
            else:
                kidx = i * tq + (a - BAND_HALF) + col
                valid = band & (kidx >= 0) & (kidx < seq)
            dst = pl.ds(r + a * dil, BAND_SUB, stride=dil) if dil > 1 else pl.ds(a, BAND_SUB)
            stats = [None, None]
            for p in range(DIL_SLABS):
                sl = slice(p * LANES, (p + 1) * LANES)
                qp = q[a:a + BAND_SUB, sl]
                kp = kk[a:a + nk, sl]
                vp = vv[a:a + nk, sl]
                res = []
                for sel in (first_head, jnp.logical_not(first_head)):
                    qm = jnp.where(sel, qp, jnp.zeros_like(qp))
                    sc = jnp.where(valid, _dot_nt(qm, kp), NEG_INF)
                    m = jnp.max(sc, axis=-1, keepdims=True)
                    e = jnp.exp2(sc - m)
                    l = jnp.sum(e, axis=-1, keepdims=True)
                    res.append((_dot(e.astype(_bf16), vp), m, l))
                o_ref[0, p, dst, :] = jnp.where(low_half, res[0][0], res[1][0])
                even = lane % (2 * LSE_LANES) < LSE_LANES
                here = lane // (2 * LSE_LANES) == p
                for j in range(2):
                    pair = jnp.where(even, res[0][1 + j], res[1][1 + j])
                    stats[j] = pair if p == 0 else jnp.where(here, pair, stats[j])
            stat_ref[0, dst, :] = jnp.where(lane % LSE_LANES < LSE_LANES // 2, stats[0], stats[1])
        return carry

    lax.fori_loop(0, dil, residue, 0, unroll=min(dil, BAND_UNROLL))


def _band_call(qkv, dil):
    _, b, _, seq, w = qkv.shape
    s = seq * dil
    tq = max(BAND_SUB, BAND_ROWS // dil)
    assert tq % BAND_SUB == 0 and seq % tq == 0
    hb = tq // BAND_HALF
    last = seq // BAND_HALF - 1

    def cur(j):
        return pl.BlockSpec((1, 1, dil, tq, w), lambda bi, i: (j, bi, 0, i, 0))

    def prev(j):
        return pl.BlockSpec((1, 1, dil, BAND_HALF, w),
                            lambda bi, i: (j, bi, 0, jnp.maximum(i * hb - 1, 0), 0))

    def nxt(j):
        return pl.BlockSpec((1, 1, dil, BAND_HALF, w),
                            lambda bi, i: (j, bi, 0, jnp.minimum((i + 1) * hb, last), 0))

    kern = functools.partial(_band_kernel, dil=dil, tq=tq, seq=seq)
    return pl.pallas_call(
        kern,
        grid=(b, seq // tq),
        in_specs=[cur(0), prev(1), cur(1), nxt(1), prev(2), cur(2), nxt(2)],
        out_specs=[pl.BlockSpec((1, DIL_SLABS, dil * tq, LANES), lambda bi, i: (bi, 0, i, 0)),
                   pl.BlockSpec((1, dil * tq, LANES), lambda bi, i: (bi, i, 0))],
        out_shape=[jax.ShapeDtypeStruct((b, DIL_SLABS, s, LANES), _f32),
                   jax.ShapeDtypeStruct((b, s, LANES), _f32)],
        compiler_params=pltpu.CompilerParams(vmem_limit_bytes=VMEM_LIMIT),
        name=f"band_attn_d{dil}",
    )(qkv, qkv, qkv, qkv, qkv, qkv, qkv)


def _out_kernel(x_ref, ot_ref, zm_ref, o0_ref, o1_ref, o2_ref, l0_ref, l1_ref, l2_ref,
                zd_ref, gt_ref, gate_ref, wpa_ref, wpb_ref, wo_ref, gf_ref, ex_ref, y_ref,
                *, final_norm):
    d = x_ref.shape[-1]
    ot = jnp.concatenate([ot_ref[0, hd, 0] for hd in range(MLA_HEADS)], axis=0)
    o_mla = jnp.transpose(ot.astype(_f32)) * zm_ref[0].astype(_f32)
    a = _dot(o_mla.astype(_bf16), wpa_ref[...])

    s0, s1, s2 = l0_ref[0], l1_ref[0], l2_ref[0]
    mx = jnp.maximum(jnp.maximum(s0, s1), s2)
    w0, w1, w2 = jnp.exp2(s0 - mx), jnp.exp2(s1 - mx), jnp.exp2(s2 - mx)
    to_max = LANES - LSE_LANES // 2
    den = (w0 * pltpu.roll(s0, to_max, axis=1) + w1 * pltpu.roll(s1, to_max, axis=1)
           + w2 * pltpu.roll(s2, to_max, axis=1))
    lane = lax.broadcasted_iota(jnp.int32, (1, LANES), 1)
    inv = jnp.where(lane % LSE_LANES < LSE_LANES // 2, 1.0 / den, 0.0)
    o_dil = None
    for w, o_ref in ((w0, o0_ref), (w1, o1_ref), (w2, o2_ref)):
        wide = _dot((w * inv).astype(_bf16), ex_ref[...])
        term = wide * jnp.concatenate([o_ref[0, p] for p in range(DIL_SLABS)], axis=1)
        o_dil = term if o_dil is None else o_dil + term
    o_dil = o_dil * zd_ref[0].astype(_f32)
    bb = _dot(o_dil.astype(_bf16), wpb_ref[...])

    gt = gt_ref[0].astype(_f32)
    u = gt[:, :d] * a + gt[:, d:] * bb
    y = x_ref[0] + gate_ref[0] * _dot(u.astype(_bf16), wo_ref[...])
    if final_norm:
        y = _rms(y, gf_ref[...])
    y_ref[0] = y


def _out_call(x, ot, zm, outs, lses, zd, gt, gate, lw, g_final, final_norm):
    b, s, d = x.shape
    tm = min(TOKEN_TILE, s)
    tok = lambda w: pl.BlockSpec((1, tm, w), lambda bi, ti: (bi, ti, 0))
    slab = pl.BlockSpec((1, DIL_SLABS, tm, LANES), lambda bi, ti: (bi, 0, ti, 0))
    stat = tok(LANES)
    const = lambda *shape: pl.BlockSpec(shape, lambda bi, ti: (0,) * len(shape))
    kern = functools.partial(_out_kernel, final_norm=final_norm)
    assert ot.shape[-1] % tm == 0
    per = ot.shape[-1] // tm
    spread = np.zeros((LANES, DIL_WIDTH), np.float32)
    for hd in range(DIL_HEADS):
        spread[hd * LSE_LANES, hd * DIL_HEAD_DIM:(hd + 1) * DIL_HEAD_DIM] = 1.0
    return pl.pallas_call(
        kern,
        grid=(b, s // tm),
        in_specs=[tok(d),
                  pl.BlockSpec((1, MLA_HEADS, 1, MLA_V, tm),
                               lambda bi, ti: (bi, 0, ti // per, 0, ti % per)),
                  tok(MLA_WIDTH),
                  slab, slab, slab, stat, stat, stat,
                  tok(DIL_WIDTH), tok(2 * d),
                  pl.BlockSpec((1, 1, d), lambda bi, ti: (bi, 0, 0)),
                  const(MLA_WIDTH, d), const(DIL_WIDTH, d), const(d, d), const(1, d),
                  const(LANES, DIL_WIDTH)],
        out_specs=tok(d),
        out_shape=jax.ShapeDtypeStruct((b, s, d), _f32),
        compiler_params=pltpu.CompilerParams(vmem_limit_bytes=VMEM_LIMIT),
        name="out_proj",
    )(x, ot, zm, outs[0], outs[1], outs[2], lses[0], lses[1], lses[2],
      zd, gt, gate, lw["w_pa"], lw["w_pb"], lw["w_out"], g_final, jnp.asarray(spread, _bf16))


def _mla_lane_order():
    half, mid = MLA_ROPE // 2, HALF_LANES
    first = mid - half
    src = np.full((HEAD_PAD,), -1, np.int64)
    src[0:half] = MLA_NOPE + np.arange(half)
    src[half:mid] = np.arange(first)
    src[mid:mid + half] = MLA_NOPE + half + np.arange(half)
    src[mid + half:mid + half + MLA_NOPE - first] = np.arange(first, MLA_NOPE)
    return src


def _gather_cols(w, src):
    picked = jnp.take(w, jnp.asarray(np.maximum(src, 0)), axis=1)
    return jnp.where(jnp.asarray(src >= 0)[None, :], picked, 0.0)


def _dil_perm():
    idx = []
    half = DIL_HEAD_DIM // 2
    for p in range(DIL_HEADS // 2):
        ha, hb = 2 * p * DIL_HEAD_DIM, (2 * p + 1) * DIL_HEAD_DIM
        idx += list(range(ha, ha + half)) + list(range(hb, hb + half))
        idx += list(range(ha + half, ha + 2 * half)) + list(range(hb + half, hb + 2 * half))
    return np.asarray(idx, np.int64)


def _pack_layer(w_in, g_norm, b_gate, g_cq, w_uq, g_ckv, w_ukv, w_pa, w_pb, w_out):
    d = w_in.shape[0]
    splits = np.cumsum([Q_LORA, KV_LORA, MLA_ROPE, MLA_WIDTH] + [DIL_WIDTH] * (3 * N_GROUPS)
                       + [DIL_WIDTH])
    parts = jnp.split(w_in, list(splits), axis=1)
    perm = _dil_perm()
    lane_src = _mla_lane_order()
    kr_src = np.where(lane_src >= MLA_NOPE, lane_src - MLA_NOPE, -1)
    cols = [parts[0], parts[1], _gather_cols(parts[2], kr_src), parts[3]]
    for j in range(3 * N_GROUPS):
        pj = parts[4 + j]
        cols.append(pj if j % 3 == 2 else jnp.take(pj, jnp.asarray(perm), axis=1))
    cols += [parts[4 + 3 * N_GROUPS], parts[5 + 3 * N_GROUPS]]
    w_big = jnp.concatenate(cols, axis=1).astype(_bf16)

    dq = MLA_NOPE + MLA_ROPE
    q_src = np.concatenate([np.where(lane_src >= 0, h * dq + lane_src, -1) for h in range(MLA_HEADS)])
    w_uq_p = jnp.transpose(_gather_cols(w_uq, q_src)).astype(_bf16)
    dkv = MLA_NOPE + MLA_V
    nope_src = np.where((lane_src >= 0) & (lane_src < MLA_NOPE), lane_src, -1)
    k_src = np.concatenate([np.where(nope_src >= 0, h * dkv + nope_src, -1) for h in range(MLA_HEADS)])
    w_k = _gather_cols(w_ukv, k_src).astype(_bf16)
    v_src = np.concatenate([h * dkv + MLA_NOPE + np.arange(MLA_V) for h in range(MLA_HEADS)])
    w_vt = jnp.transpose(jnp.take(w_ukv, jnp.asarray(v_src), axis=1)).astype(_bf16)

    return dict(
        w_big=w_big, g_norm=g_norm.reshape(1, d), b_gate=b_gate.reshape(1, -1),
        g_cq=g_cq.reshape(1, -1), w_uq=w_uq_p, g_ckv=g_ckv.reshape(1, -1), w_k=w_k,
        w_vt=w_vt,
        w_pa=w_pa.astype(_bf16), w_pb=w_pb.astype(_bf16), w_out=w_out.astype(_bf16))


def _rope_tables(s, tm):
    pos = np.arange(s, dtype=np.float64)[:, None]
    lane = np.arange(LANES)

    def table(freq_idx, active, n_freq):
        inv = np.power(ROPE_THETA, -2.0 * freq_idx / (2 * n_freq))
        ang = pos * inv[None, :]
        cos = np.where(active[None, :], np.cos(ang), 1.0)
        sign = np.where(lane < HALF_LANES, -1.0, 1.0)
        sin = np.where(active[None, :], np.sin(ang) * sign[None, :], 0.0)
        return np.stack([cos, sin]).astype(np.float32)

    half = MLA_ROPE // 2
    in_half = lane % HALF_LANES
    tab_m = table(in_half.clip(0, half - 1).astype(np.float64), in_half < half, half)
    n_dil = DIL_HEAD_DIM // 2
    tab_d = jnp.asarray(table((lane % n_dil).astype(np.float64), np.ones(LANES, bool), n_dil))
    tabs = []
    for _, dil in DIL_GROUPS:
        t = tab_d.reshape(2, s // tm, tm // dil, dil, LANES)
        tabs.append(jnp.swapaxes(t, 2, 3).reshape(2, s, LANES))
    return jnp.asarray(tab_m), tabs


def _trunk(x, mods, layers, g_final):
    b, s, d = x.shape
    assert s % (DIL_GROUPS[-1][1] * BAND_SUB) == 0 and s % TOKEN_TILE == 0
    tab_m, tabs_d = _rope_tables(s, min(TOKEN_TILE, s))
    depth = len(layers)
    for l, lw in enumerate(layers):
        mod = mods[l]
        shift, scale, gate = [mod[:, None, j * d:(j + 1) * d] for j in range(3)]
        q, k, vt, zm, d1, d4, d16, zd, gt = _in_call(x, scale, shift, lw, tab_m, tabs_d)
        ot = _mla_call(q, k, vt)
        outs, lses = [], []
        for qkv, (_, dl) in zip((d1, d4, d16), DIL_GROUPS):
            o_g, lse_g = _band_call(qkv, dl)
            outs.append(o_g)
            lses.append(lse_g)
        x = _out_call(x, ot, zm, outs, lses, zd, gt, gate, lw, g_final.reshape(1, d),
                      final_norm=(l == depth - 1))
    return x


def kernel(x_prompt, x_sample, c_prompt, c_sample, w_ada, b_ada, g_norm, w_in, b_gate, g_cq, w_uq,
           g_ckv, w_ukv, w_pa, w_pb, w_out, g_final):
    depth = w_in.shape[0]
    bp, bs = c_prompt.shape[0], c_sample.shape[0]
    rows = -(-(bp + bs) // 8) * 8
    c_all = jnp.concatenate([c_prompt, c_sample], axis=0)
    c_all = jnp.pad(c_all, ((0, rows - bp - bs), (0, 0)))
    mods = _ada_call(c_all, w_ada, b_ada)
    layers = [_pack_layer(w_in[l], g_norm[l], b_gate[l], g_cq[l], w_uq[l], g_ckv[l], w_ukv[l],
                          w_pa[l], w_pb[l], w_out[l]) for l in range(depth)]
    y_prompt = _trunk(x_prompt, mods[:, :bp], layers, g_final)
    y_sample = _trunk(x_sample, mods[:, bp:bp + bs], layers, g_final)
    return (y_prompt, y_sample)
```
